```python
import math
import jax
import jax.numpy as jnp
from jax import lax
import numpy as np

D_MODEL = 1024
BATCH = 2
SEQ = 8192
DEPTH = 4
DEC_BATCH = 32
DEC_SEQ = 4
PAST_LEN = 8192
PAGE_SIZE = 128

N_MIXERS = 3
LAYER_KIND = tuple(i % N_MIXERS for i in range(DEPTH))
KIND_INDEX = tuple(LAYER_KIND[:i].count(LAYER_KIND[i]) for i in range(DEPTH))
N_LRU = LAYER_KIND.count(0)
N_GDN = LAYER_KIND.count(1)
N_SWA = LAYER_KIND.count(2)

DN_ALPHA = (2 * DEPTH) ** 0.25
DN_BETA = (8 * DEPTH) ** -0.25
LN_EPS = 1e-5
NORM_EPS = 1e-6
CONV_WIDTH = 4

D_RNN = D_MODEL
LRU_BLOCKS = 4
LRU_BW = D_RNN // LRU_BLOCKS
LRU_C = 8.0

GDN_HK = 8
GDN_HV = 16
GDN_DK = 128
GDN_DV = 128
GDN_QK = GDN_HK * GDN_DK
GDN_VD = GDN_HV * GDN_DV
GDN_CONV_DIM = 2 * GDN_QK + GDN_VD
GDN_IN = GDN_CONV_DIM + GDN_VD + 2 * GDN_HV
GDN_CHUNK = 64

SWA_WINDOWS = (128, 512, 2048)
SWA_DILATIONS = (1, 4, 16)
SWA_GROUPS = 3
SWA_HEADS = 16
SWA_HD = D_MODEL // SWA_HEADS
SWA_WIDTH = SWA_HEADS * SWA_HD
SWA_IN = SWA_GROUPS * 3 * SWA_WIDTH
SWA_BLOCK = 128
ROPE_THETA = 10000.0

N_EXPERTS = 32
N_EXPERT_GROUPS = 4
EXPERTS_PER_GROUP = N_EXPERTS // N_EXPERT_GROUPS
TOP_K = 2
D_FF_EXPERT = D_MODEL // 2
MOE_BLOCK = 128

kernel_name = 'hybrid_lru_gdn_dilated_swa_moe_step'


def _layer_norm(x, g, b):
    xf = x.astype(jnp.float32)
    mu = jnp.mean(xf, -1, keepdims=True)
    var = jnp.mean(jnp.square(xf - mu), -1, keepdims=True)
    y = (xf - mu) * lax.rsqrt(var + LN_EPS) * g.astype(jnp.float32) + b.astype(jnp.float32)
    return y.astype(x.dtype)


def _causal_conv(u, buf, w):
    C = u.shape[-1]
    xp = jnp.concatenate([buf.astype(u.dtype), u], axis=1)
    y = lax.conv_general_dilated(xp, w.astype(u.dtype)[:, None, :], window_strides=(1,), padding='VALID',
                                 dimension_numbers=('NWC', 'WIO', 'NWC'), feature_group_count=C)
    return y, xp[:, -(CONV_WIDTH - 1):]


def _rope(x, pos):
    hd = x.shape[-1]
    half = hd // 2
    inv = ROPE_THETA ** (-2.0 * jnp.arange(half, dtype=jnp.float32) / hd)
    ang = pos.astype(jnp.float32)[:, None] * inv[None, :]
    bshape = (ang.shape[0],) + (1,) * (x.ndim - 3) + (half,)
    cos = jnp.cos(ang).reshape(bshape)
    sin = jnp.sin(ang).reshape(bshape)
    xf = x.astype(jnp.float32)
    x1, x2 = xf[..., :half], xf[..., half:]
    return jnp.concatenate([x1 * cos - x2 * sin, x2 * cos + x1 * sin], -1).astype(x.dtype)


def _linear_scan(a, b, h0):
    def combine(left, right):
        a_l, b_l = left
        a_r, b_r = right
        return a_l * a_r, a_r * b_l + b_r
    a_cum, b_cum = lax.associative_scan(combine, (a, b), axis=1)
    return a_cum * h0[:, None, :] + b_cum


def _lru_mixer(x, h0, conv_buf, w_in, b_in, conv_w, conv_b, w_a, b_a, w_i, b_i, lam, w_out):
    B, L, _ = x.shape
    proj = x @ w_in + b_in
    gate, u = proj[..., :D_RNN], proj[..., D_RNN:]
    u, new_buf = _causal_conv(u, conv_buf, conv_w)
    u = u + conv_b
    ub = u.reshape(B, L, LRU_BLOCKS, LRU_BW)
    r = jax.nn.sigmoid((jnp.einsum('blnc,ncd->blnd', ub, w_a).reshape(B, L, D_RNN) + b_a).astype(jnp.float32))
    i = jax.nn.sigmoid((jnp.einsum('blnc,ncd->blnd', ub, w_i).reshape(B, L, D_RNN) + b_i).astype(jnp.float32))
    log_a = -LRU_C * r * jax.nn.softplus(-lam.astype(jnp.float32))
    a = jnp.exp(log_a)
    xin = jnp.sqrt(-jnp.expm1(2.0 * log_a)) * i * u.astype(jnp.float32)
    h = _linear_scan(a, xin, h0.astype(jnp.float32))
    y = (h.astype(x.dtype) * jax.nn.gelu(gate)) @ w_out
    return y, h[:, -1].astype(x.dtype), new_buf


def _l2norm(t):
    return t * lax.rsqrt(jnp.sum(t * t, -1, keepdims=True) + NORM_EPS)


def _gated_delta_rule(q, k, v, g, beta, s0):
    B, L, H, DK = q.shape
    DV = v.shape[-1]
    C = math.gcd(L, GDN_CHUNK)
    n = L // C

    def blk(t):
        return t.reshape((B, n, C, H) + t.shape[3:]).swapaxes(2, 3).swapaxes(0, 1)

    qc, kc, vc, gch, bc = blk(q), blk(k), blk(v), blk(g), blk(beta)
    gcum = jnp.cumsum(gch, axis=-1)
    idx = jnp.arange(C)
    tril = idx[:, None] >= idx[None, :]
    strict = idx[:, None] > idx[None, :]
    decay = jnp.exp(jnp.where(tril, gcum[..., :, None] - gcum[..., None, :], -jnp.inf))
    kk = jnp.einsum('nbhid,nbhjd->nbhij', kc, kc)
    a_mat = jnp.where(strict, bc[..., :, None] * kk * decay, 0.0)
    eye = jnp.eye(C, dtype=jnp.float32)
    t_mat = lax.linalg.triangular_solve(eye + a_mat, jnp.broadcast_to(eye, a_mat.shape), left_side=True, lower=True)
    u_c = t_mat @ (vc * bc[..., None])
    w_c = t_mat @ (kc * (bc * jnp.exp(gcum))[..., None])
    qk = jnp.einsum('nbhid,nbhjd->nbhij', qc, kc) * decay
    qg = qc * jnp.exp(gcum)[..., None]
    kg = kc * jnp.exp(gcum[..., -1:] - gcum)[..., None]
    glast = jnp.exp(gcum[..., -1])

    def step(s, inp):
        u_, w_, qk_, qg_, kg_, gl_ = inp
        v_new = u_ - jnp.einsum('bhck,bhkv->bhcv', w_, s)
        o = jnp.einsum('bhck,bhkv->bhcv', qg_, s) + jnp.einsum('bhij,bhjv->bhiv', qk_, v_new)
        s = s * gl_[..., None, None] + jnp.einsum('bhck,bhcv->bhkv', kg_, v_new)
        return s, o

    s_fin, o = lax.scan(step, s0, (u_c, w_c, qk, qg, kg, glast))
    o = o.swapaxes(0, 1).swapaxes(2, 3).reshape(B, L, H, DV)
    return o, s_fin


def _gdn_mixer(x, s0, conv_buf, w_in, conv_w, a_log, dt_bias, norm_w, w_out):
    B, L, _ = x.shape
    proj = x @ w_in
    qkv = proj[..., :GDN_CONV_DIM]
    z = proj[..., GDN_CONV_DIM:GDN_CONV_DIM + GDN_VD]
    b_raw = proj[..., GDN_CONV_DIM + GDN_VD:GDN_CONV_DIM + GDN_VD + GDN_HV]
    a_raw = proj[..., GDN_CONV_DIM + GDN_VD + GDN_HV:]
    qkv, new_buf = _causal_conv(qkv, conv_buf, conv_w)
    qkv = jax.nn.silu(qkv.astype(jnp.float32))
    q = qkv[..., :GDN_QK].reshape(B, L, GDN_HK, GDN_DK)
    k = qkv[..., GDN_QK:2 * GDN_QK].reshape(B, L, GDN_HK, GDN_DK)
    v = qkv[..., 2 * GDN_QK:].reshape(B, L, GDN_HV, GDN_DV)
    rep = GDN_HV // GDN_HK
    q = jnp.repeat(_l2norm(q) * (GDN_DK ** -0.5), rep, axis=2)
    k = jnp.repeat(_l2norm(k), rep, axis=2)
    beta = jax.nn.sigmoid(b_raw.astype(jnp.float32))
    g = -jnp.exp(a_log.astype(jnp.float32)) * jax.nn.softplus(a_raw.astype(jnp.float32) + dt_bias.astype(jnp.float32))
    o, s = _gated_delta_rule(q, k, v, g, beta, s0.astype(jnp.float32))
    o = o * lax.rsqrt(jnp.mean(o * o, -1, keepdims=True) + NORM_EPS) * norm_w.astype(jnp.float32)
    o = o * jax.nn.silu(z.astype(jnp.float32)).reshape(B, L, GDN_HV, GDN_DV)
    y = o.reshape(B, L, GDN_VD).astype(x.dtype) @ w_out
    return y, s.astype(x.dtype), new_buf


def _swa_project(x, pos, w_in):
    B, L, _ = x.shape
    p = (x @ w_in).reshape(B, L, SWA_GROUPS, 3, SWA_HEADS, SWA_HD)
    return _rope(p[:, :, :, 0], pos), _rope(p[:, :, :, 1], pos), p[:, :, :, 2]


def _swa_prompt_group(q, k, v, window, dilation):
    B, S, H, hd = q.shape
    nw = window // dilation
    QB = SWA_BLOCK
    span = dilation * QB
    Sp = -(-S // span) * span
    M = Sp // dilation
    nb = M // QB

    def to_blocks(t):
        t = jnp.pad(t, ((0, 0), (0, Sp - S), (0, 0), (0, 0)))
        t = t.reshape(B, M, dilation, H, hd).transpose(0, 2, 1, 3, 4)
        return t.reshape(B, dilation, nb, QB, H, hd)

    def with_prev(t):
        prev = jnp.pad(t, ((0, 0), (0, 0), (1, 0), (0, 0), (0, 0), (0, 0)))[:, :, :-1]
        return jnp.concatenate([prev, t], axis=3)

    qb = to_blocks(q)
    kc = with_prev(to_blocks(k))
    vc = with_prev(to_blocks(v))
    s = jnp.einsum('brnqhe,brnkhe->brnhqk', qb, kc).astype(jnp.float32) * (hd ** -0.5)
    qi = jnp.arange(QB)[:, None]
    ki = jnp.arange(2 * QB)[None, :]
    delta = QB + qi - ki
    band = (delta >= 0) & (delta <= nw)
    exists = (jnp.arange(nb)[:, None, None] > 0) | (ki[None] >= QB)
    mask = band[None] & exists
    s = jnp.where(mask[None, None, :, None], s, -jnp.inf)
    lse = jax.nn.logsumexp(s, axis=-1)
    p = jnp.exp(s - lse[..., None])
    o = jnp.einsum('brnhqk,brnkhe->brnqhe', p.astype(v.dtype), vc)
    o = o.reshape(B, dilation, M, H, hd).transpose(0, 2, 1, 3, 4).reshape(B, Sp, H, hd)[:, :S]
    lse = lse.transpose(0, 1, 2, 4, 3).reshape(B, dilation, M, H).transpose(0, 2, 1, 3).reshape(B, Sp, H)[:, :S]
    return o, lse


def _swa_sample_group(q, k, v, kbuf, vbuf, window, dilation):
    Bd, L, H, hd = q.shape
    Wb = kbuf.shape[1]
    J = window // dilation + 1
    kc = jnp.concatenate([kbuf.astype(k.dtype), k], 1)
    vc = jnp.concatenate([vbuf.astype(v.dtype), v], 1)
    idx = Wb + jnp.arange(L)[:, None] - dilation * jnp.arange(J)[None, :]
    valid = idx >= 0
    idx = jnp.maximum(idx, 0)
    kg = kc[:, idx]
    vg = vc[:, idx]
    s = jnp.einsum('blhe,bljhe->blhj', q, kg).astype(jnp.float32) * (hd ** -0.5)
    s = jnp.where(valid[None, :, None, :], s, -jnp.inf)
    lse = jax.nn.logsumexp(s, axis=-1)
    p = jnp.exp(s - lse[..., None])
    o = jnp.einsum('blhj,bljhe->blhe', p.astype(v.dtype), vg)
    return o, lse


def _merge_groups(outs, lses, w_out):
    wts = jax.nn.softmax(jnp.stack(lses, 0), axis=0)
    o = jnp.sum(wts[..., None] * jnp.stack(outs, 0).astype(jnp.float32), axis=0)
    B, L = o.shape[:2]
    return o.reshape(B, L, SWA_WIDTH).astype(w_out.dtype) @ w_out


def _swa_prompt(x, w_in, w_out):
    B, S, _ = x.shape
    q, k, v = _swa_project(x, jnp.arange(S), w_in)
    outs, lses, new_cache = [], [], []
    for g in range(SWA_GROUPS):
        o, lse = _swa_prompt_group(q[:, :, g], k[:, :, g], v[:, :, g], SWA_WINDOWS[g], SWA_DILATIONS[g])
        outs.append(o)
        lses.append(lse)
        keep = min(SWA_WINDOWS[g], S)
        new_cache.append(jnp.stack([k[:, S - keep:, g], v[:, S - keep:, g]], axis=2))
    return _merge_groups(outs, lses, w_out), new_cache


def _swa_sample(x, caches, w_in, w_out):
    B, L, _ = x.shape
    q, k, v = _swa_project(x, PAST_LEN + jnp.arange(L), w_in)
    outs, lses, new_rows = [], [], []
    for g in range(SWA_GROUPS):
        cache = caches[g]
        o, lse = _swa_sample_group(q[:, :, g], k[:, :, g], v[:, :, g], cache[:, :, 0], cache[:, :, 1],
                                   SWA_WINDOWS[g], SWA_DILATIONS[g])
        outs.append(o)
        lses.append(lse)
        new_rows.append(jnp.stack([k[:, :, g], v[:, :, g]], axis=2))
    return _merge_groups(outs, lses, w_out), new_rows


def _route(x2, router_w, router_b):
    logits = (x2 @ router_w).astype(jnp.float32) + router_b.astype(jnp.float32)
    probs = jax.nn.softmax(logits, axis=-1)
    T = probs.shape[0]
    top2_in_group = lax.top_k(probs.reshape(T, N_EXPERT_GROUPS, EXPERTS_PER_GROUP), 2)[0]
    group = jnp.argmax(jnp.sum(top2_in_group, -1), -1)
    in_group = (jnp.arange(N_EXPERTS) // EXPERTS_PER_GROUP)[None, :] == group[:, None]
    w_top, idx = lax.top_k(jnp.where(in_group, probs, -1.0), TOP_K)
    w_top = w_top / jnp.sum(w_top, -1, keepdims=True)
    return idx, w_top


def _moe(x, router_w, router_b, w_gate, w_up, w_down):
    B, L, D = x.shape
    T = B * L
    x2 = x.reshape(T, D)
    idx, wt = _route(x2, router_w, router_b)
    n_assign = T * TOP_K
    flat_e = idx.reshape(-1)
    flat_w = wt.reshape(-1)
    flat_t = jnp.arange(n_assign, dtype=jnp.int32) // TOP_K
    order = jnp.argsort(flat_e)
    se, st, sw = flat_e[order], flat_t[order], flat_w[order]
    counts = jnp.bincount(flat_e, length=N_EXPERTS)
    padded = (counts + MOE_BLOCK - 1) // MOE_BLOCK * MOE_BLOCK
    pend = jnp.cumsum(padded)
    pstart = pend - padded
    cstart = jnp.cumsum(counts) - counts
    dest = pstart[se] + jnp.arange(n_assign) - cstart[se]
    n_blocks = (n_assign + N_EXPERTS * (MOE_BLOCK - 1) + MOE_BLOCK - 1) // MOE_BLOCK
    n_rows = n_blocks * MOE_BLOCK
    row_tok = jnp.full((n_rows,), T, jnp.int32).at[dest].set(st)
    row_w = jnp.zeros((n_rows,), jnp.float32).at[dest].set(sw)
    blk_e = jnp.minimum(jnp.searchsorted(pend, jnp.arange(n_blocks) * MOE_BLOCK, side='right'), N_EXPERTS - 1)
    xb = jnp.concatenate([x2, jnp.zeros((1, D), x.dtype)], 0)[row_tok].reshape(n_blocks, MOE_BLOCK, D)

    def expert_block(args):
        xblk, e = args
        hid = jax.nn.silu(xblk @ w_gate[e]) * (xblk @ w_up[e])
        return hid @ w_down[e]

    yb = lax.map(expert_block, (xb, blk_e))
    y = jax.ops.segment_sum(yb.reshape(n_rows, D) * row_w[:, None].astype(yb.dtype), row_tok,
                            num_segments=T + 1)[:T]
    return y.reshape(B, L, D)


def setup_inputs(seed: int = 0) -> dict:
    key = jax.random.key(seed)
    ks = iter(jax.random.split(key, 64))
    f32 = jnp.float32

    def nrm(shape, scale):
        return jax.random.normal(next(ks), shape, f32) * scale

    def unif(shape, lo, hi):
        return jax.random.uniform(next(ks), shape, f32, lo, hi)

    x_prompt = nrm((BATCH, SEQ, D_MODEL), 1.0)
    x_sample = nrm((DEC_BATCH, DEC_SEQ, D_MODEL), 1.0)
    state_lru_h = nrm((N_LRU, DEC_BATCH, D_RNN), 0.5)
    state_lru_conv = nrm((N_LRU, DEC_BATCH, CONV_WIDTH - 1, D_RNN), 1.0)
    state_gdn_s = nrm((N_GDN, DEC_BATCH, GDN_HV, GDN_DK, GDN_DV), 0.1)
    state_gdn_conv = nrm((N_GDN, DEC_BATCH, CONV_WIDTH - 1, GDN_CONV_DIM), 1.0)
    cache_swa_w128 = nrm((N_SWA, DEC_BATCH, min(SWA_WINDOWS[0], PAST_LEN), 2, SWA_HEADS, SWA_HD), 1.0)
    cache_swa_w512 = nrm((N_SWA, DEC_BATCH, min(SWA_WINDOWS[1], PAST_LEN), 2, SWA_HEADS, SWA_HD), 1.0)
    cache_swa_w2048 = nrm((N_SWA, DEC_BATCH, min(SWA_WINDOWS[2], PAST_LEN), 2, SWA_HEADS, SWA_HD), 1.0)

    lru_w_in = nrm((N_LRU, D_MODEL, 2 * D_RNN), D_MODEL ** -0.5)
    lru_b_in = nrm((N_LRU, 2 * D_RNN), 0.01)
    lru_conv_w = nrm((N_LRU, CONV_WIDTH, D_RNN), CONV_WIDTH ** -0.5)
    lru_conv_b = nrm((N_LRU, D_RNN), 0.01)
    lru_w_a = nrm((N_LRU, LRU_BLOCKS, LRU_BW, LRU_BW), LRU_BW ** -0.5)
    lru_b_a = nrm((N_LRU, D_RNN), 0.01)
    lru_w_i = nrm((N_LRU, LRU_BLOCKS, LRU_BW, LRU_BW), LRU_BW ** -0.5)
    lru_b_i = nrm((N_LRU, D_RNN), 0.01)
    a_pow_c = unif((N_LRU, D_RNN), 0.9, 0.999)
    a0 = a_pow_c ** (1.0 / LRU_C)
    lru_lambda = jnp.log(a0) - jnp.log1p(-a0)
    lru_w_out = nrm((N_LRU, D_RNN, D_MODEL), D_RNN ** -0.5 * DN_BETA)

    gdn_w_in = nrm((N_GDN, D_MODEL, GDN_IN), D_MODEL ** -0.5)
    gdn_conv_w = nrm((N_GDN, CONV_WIDTH, GDN_CONV_DIM), CONV_WIDTH ** -0.5)
    gdn_a_log = jnp.log(unif((N_GDN, GDN_HV), 1.0, 16.0))
    dt = jnp.exp(unif((N_GDN, GDN_HV), math.log(1e-3), math.log(1e-1)))
    gdn_dt_bias = dt + jnp.log(-jnp.expm1(-dt))
    gdn_norm_w = 1.0 + nrm((N_GDN, GDN_DV), 0.01)
    gdn_w_out = nrm((N_GDN, GDN_VD, D_MODEL), GDN_VD ** -0.5 * DN_BETA)

    swa_w_in = nrm((N_SWA, D_MODEL, SWA_IN), D_MODEL ** -0.5)
    swa_w_out = nrm((N_SWA, SWA_WIDTH, D_MODEL), SWA_WIDTH ** -0.5 * DN_BETA)

    ln1_g = 1.0 + nrm((DEPTH, D_MODEL), 0.01)
    ln1_b = nrm((DEPTH, D_MODEL), 0.01)
    ln2_g = 1.0 + nrm((DEPTH, D_MODEL), 0.01)
    ln2_b = nrm((DEPTH, D_MODEL), 0.01)

    router_w = nrm((D_MODEL, N_EXPERTS), D_MODEL ** -0.5)
    router_b = nrm((N_EXPERTS,), 0.01)
    moe_w_gate = nrm((DEPTH, N_EXPERTS, D_MODEL, D_FF_EXPERT), D_MODEL ** -0.5)
    moe_w_up = nrm((DEPTH, N_EXPERTS, D_MODEL, D_FF_EXPERT), D_MODEL ** -0.5)
    moe_w_down = nrm((DEPTH, N_EXPERTS, D_FF_EXPERT, D_MODEL), D_FF_EXPERT ** -0.5 * DN_BETA)

    return {
        'x_prompt': x_prompt, 'x_sample': x_sample,
        'state_lru_h': state_lru_h, 'state_lru_conv': state_lru_conv,
        'state_gdn_s': state_gdn_s, 'state_gdn_conv': state_gdn_conv,
        'cache_swa_w128': cache_swa_w128, 'cache_swa_w512': cache_swa_w512, 'cache_swa_w2048': cache_swa_w2048,
        'lru_w_in': lru_w_in, 'lru_b_in': lru_b_in, 'lru_conv_w': lru_conv_w, 'lru_conv_b': lru_conv_b,
        'lru_w_a': lru_w_a, 'lru_b_a': lru_b_a, 'lru_w_i': lru_w_i, 'lru_b_i': lru_b_i,
        'lru_lambda': lru_lambda, 'lru_w_out': lru_w_out,
        'gdn_w_in': gdn_w_in, 'gdn_conv_w': gdn_conv_w, 'gdn_a_log': gdn_a_log, 'gdn_dt_bias': gdn_dt_bias,
        'gdn_norm_w': gdn_norm_w, 'gdn_w_out': gdn_w_out,
        'swa_w_in': swa_w_in, 'swa_w_out': swa_w_out,
        'ln1_g': ln1_g, 'ln1_b': ln1_b, 'ln2_g': ln2_g, 'ln2_b': ln2_b,
        'router_w': router_w, 'router_b': router_b,
        'moe_w_gate': moe_w_gate, 'moe_w_up': moe_w_up, 'moe_w_down': moe_w_down,
    }


def reference(x_prompt, x_sample, state_lru_h, state_lru_conv, state_gdn_s, state_gdn_conv,
              cache_swa_w128, cache_swa_w512, cache_swa_w2048,
              lru_w_in, lru_b_in, lru_conv_w, lru_conv_b, lru_w_a, lru_b_a, lru_w_i, lru_b_i,
              lru_lambda, lru_w_out,
              gdn_w_in, gdn_conv_w, gdn_a_log, gdn_dt_bias, gdn_norm_w, gdn_w_out,
              swa_w_in, swa_w_out,
              ln1_g, ln1_b, ln2_g, ln2_b,
              router_w, router_b, moe_w_gate, moe_w_up, moe_w_down):
    swa_caches = (cache_swa_w128, cache_swa_w512, cache_swa_w2048)
    xp, xs = x_prompt, x_sample
    B = xp.shape[0]
    lru_h_p, lru_h_s, lru_c_p, lru_c_s = [], [], [], []
    gdn_s_p, gdn_s_s, gdn_c_p, gdn_c_s = [], [], [], []
    swa_p = [[] for _ in range(SWA_GROUPS)]
    swa_s = [[] for _ in range(SWA_GROUPS)]
    for i in range(DEPTH):
        kind, j = LAYER_KIND[i], KIND_INDEX[i]
        if kind == 0:
            params = (lru_w_in[j], lru_b_in[j], lru_conv_w[j], lru_conv_b[j], lru_w_a[j], lru_b_a[j],
                      lru_w_i[j], lru_b_i[j], lru_lambda[j], lru_w_out[j])
            yp, hp, cp = _lru_mixer(xp, jnp.zeros((B, D_RNN), xp.dtype),
                                    jnp.zeros((B, CONV_WIDTH - 1, D_RNN), xp.dtype), *params)
            ys, hs, cs = _lru_mixer(xs, state_lru_h[j], state_lru_conv[j], *params)
            lru_h_p.append(hp)
            lru_h_s.append(hs)
            lru_c_p.append(cp)
            lru_c_s.append(cs)
        elif kind == 1:
            params = (gdn_w_in[j], gdn_conv_w[j], gdn_a_log[j], gdn_dt_bias[j], gdn_norm_w[j], gdn_w_out[j])
            yp, sp, cp = _gdn_mixer(xp, jnp.zeros((B, GDN_HV, GDN_DK, GDN_DV), jnp.float32),
                                    jnp.zeros((B, CONV_WIDTH - 1, GDN_CONV_DIM), xp.dtype), *params)
            ys, ss, cs = _gdn_mixer(xs, state_gdn_s[j], state_gdn_conv[j], *params)
            gdn_s_p.append(sp)
            gdn_s_s.append(ss)
            gdn_c_p.append(cp)
            gdn_c_s.append(cs)
        else:
            yp, newp = _swa_prompt(xp, swa_w_in[j], swa_w_out[j])
            ys, news = _swa_sample(xs, tuple(c[j] for c in swa_caches), swa_w_in[j], swa_w_out[j])
            for g in range(SWA_GROUPS):
                swa_p[g].append(newp[g])
                swa_s[g].append(news[g])
        xp = _layer_norm(DN_ALPHA * xp + yp, ln1_g[i], ln1_b[i])
        xs = _layer_norm(DN_ALPHA * xs + ys, ln1_g[i], ln1_b[i])
        xp = _layer_norm(DN_ALPHA * xp + _moe(xp, router_w, router_b, moe_w_gate[i], moe_w_up[i], moe_w_down[i]),
                         ln2_g[i], ln2_b[i])
        xs = _layer_norm(DN_ALPHA * xs + _moe(xs, router_w, router_b, moe_w_gate[i], moe_w_up[i], moe_w_down[i]),
                         ln2_g[i], ln2_b[i])
    lru_h_prompt = jnp.stack(lru_h_p)
    lru_h_sample = jnp.stack(lru_h_s)
    lru_conv_prompt = jnp.stack(lru_c_p)
    lru_conv_sample = jnp.stack(lru_c_s)
    gdn_s_prompt = jnp.stack(gdn_s_p)
    gdn_s_sample = jnp.stack(gdn_s_s)
    gdn_conv_prompt = jnp.stack(gdn_c_p)
    gdn_conv_sample = jnp.stack(gdn_c_s)
    swa_w128_prompt = jnp.stack(swa_p[0])
    swa_w128_sample = jnp.stack(swa_s[0])
    swa_w512_prompt = jnp.stack(swa_p[1])
    swa_w512_sample = jnp.stack(swa_s[1])
    swa_w2048_prompt = jnp.stack(swa_p[2])
    swa_w2048_sample = jnp.stack(swa_s[2])
    return (xp, xs, lru_h_prompt, lru_h_sample, lru_conv_prompt, lru_conv_sample,
            gdn_s_prompt, gdn_s_sample, gdn_conv_prompt, gdn_conv_sample,
            swa_w128_prompt, swa_w128_sample, swa_w512_prompt, swa_w512_sample,
            swa_w2048_prompt, swa_w2048_sample)
```

```python
import functools
import math

import jax
import jax.numpy as jnp
from jax import lax
from jax.experimental import pallas as pl
from jax.experimental.pallas import tpu as pltpu

F32 = jnp.float32
BF16 = jnp.bfloat16
I32 = jnp.int32
HIGHEST = lax.Precision.HIGHEST

DEPTH = 4
LAYER_KIND = (0, 1, 2, 0)
KIND_INDEX = (0, 0, 0, 1)
DN_ALPHA = (2 * DEPTH) ** 0.25
LN_EPS = 1e-5
NORM_EPS = 1e-6
CONV_WIDTH = 4
LRU_BLOCKS = 4
LRU_C = 8.0
GDN_HK = 8
GDN_HV = 16
GDN_DK = 128
GDN_DV = 128
GDN_CHUNK = 64
SWA_WINDOWS = (128, 512, 2048)
SWA_DILATIONS = (1, 4, 16)
SWA_GROUPS = 3
SWA_HEADS = 16
SWA_HD = 64
SWA_BLOCK = 128
ROPE_THETA = 10000.0
PAST_LEN = 8192
N_EXPERTS = 32
N_EXPERT_GROUPS = 4
EXPERTS_PER_GROUP = 8

LANES = 128
SUBLANES = 8
SAMPLE_PAD = 8
SWA_TILE = 2048
MOE_ROWS = 256
TOK_BLK = 128
VMEM_LIMIT = 56 * 1024 * 1024


def _cparams(sem):
    return pltpu.CompilerParams(dimension_semantics=sem, vmem_limit_bytes=VMEM_LIMIT)


def _pick(n, cands):
    for c in cands:
        if n % c == 0:
            return c
    raise ValueError(f"no tile for {n}")


def _bdot(a, b):
    return jnp.dot(a.astype(BF16), b.astype(BF16), preferred_element_type=F32)


def _nt(a, b):
    return lax.dot_general(a.astype(BF16), b.astype(BF16), (((1,), (1,)), ((), ())),
                           preferred_element_type=F32)


def _tn(a, b):
    return lax.dot_general(a.astype(BF16), b.astype(BF16), (((0,), (0,)), ((), ())),
                           preferred_element_type=F32)


def _hdot(a, b):
    return jnp.dot(a, b, precision=HIGHEST, preferred_element_type=F32)


def _sigmoid(x):
    return 1.0 / (1.0 + jnp.exp(-x))


def _softplus(x):
    return jnp.maximum(x, 0.0) + jnp.log(1.0 + jnp.exp(-jnp.abs(x)))


def _layer_norm(x, g, b):
    mu = jnp.mean(x, -1, keepdims=True)
    xc = x - mu
    var = jnp.mean(xc * xc, -1, keepdims=True)
    return xc * lax.rsqrt(var + LN_EPS) * g + b


def _row_scan_sum(x, n):
    rows = lax.broadcasted_iota(I32, x.shape, 0)
    s = 1
    while s < n:
        x = x + jnp.where(rows >= s, pltpu.roll(x, s, 0), 0.0)
        s *= 2
    return x


def _proj_kernel(x_ref, w_ref, b_ref, o_ref):
    o_ref[...] = jnp.dot(x_ref[...].astype(BF16), w_ref[...], preferred_element_type=F32) + b_ref[...]


def _proj(x, w_bf, b, name):
    M, K = x.shape
    N = w_bf.shape[1]
    tm = _pick(M, (1280, 1024, 512, 256, 128, 64, 8))
    tn = _pick(N, (1024, 768, 512, 384, 256, 128))
    return pl.pallas_call(
        _proj_kernel, grid=(M // tm, N // tn),
        in_specs=[pl.BlockSpec((tm, K), lambda i, j: (i, 0)),
                  pl.BlockSpec((K, tn), lambda i, j: (0, j)),
                  pl.BlockSpec((1, tn), lambda i, j: (0, j))],
        out_specs=pl.BlockSpec((tm, tn), lambda i, j: (i, j)),
        out_shape=jax.ShapeDtypeStruct((M, N), F32),
        compiler_params=_cparams(("parallel", "parallel")), name=name)(x, w_bf, b)


def _proj_hi_kernel(x_ref, w_ref, o_ref):
    o_ref[...] = _hdot(x_ref[...], w_ref[...])


def _proj_hi(x, w, name):
    M, K = x.shape
    N = w.shape[1]
    tm = _pick(M, (1280, 1024, 512, 256, 128, 64, 8))
    return pl.pallas_call(
        _proj_hi_kernel, grid=(M // tm,),
        in_specs=[pl.BlockSpec((tm, K), lambda i: (i, 0)), pl.BlockSpec((K, N), lambda i: (0, 0))],
        out_specs=pl.BlockSpec((tm, N), lambda i: (i, 0)),
        out_shape=jax.ShapeDtypeStruct((M, N), F32),
        compiler_params=_cparams(("parallel",)), name=name)(x, w)


def _outproj_ln_kernel(h_ref, w_ref, r_ref, g_ref, b_ref, o_ref):
    y = jnp.dot(h_ref[...].astype(BF16), w_ref[...], preferred_element_type=F32)
    o_ref[...] = _layer_norm(DN_ALPHA * r_ref[...] + y, g_ref[...], b_ref[...])


def _outproj_ln(h, w_bf, res, g, b, name):
    M, K = h.shape
    N = w_bf.shape[1]
    tm = _pick(M, (1280, 1024, 512, 256, 128, 64, 8))
    return pl.pallas_call(
        _outproj_ln_kernel, grid=(M // tm,),
        in_specs=[pl.BlockSpec((tm, K), lambda i: (i, 0)),
                  pl.BlockSpec((K, N), lambda i: (0, 0)),
                  pl.BlockSpec((tm, N), lambda i: (i, 0)),
                  pl.BlockSpec((1, N), lambda i: (0, 0)),
                  pl.BlockSpec((1, N), lambda i: (0, 0))],
        out_specs=pl.BlockSpec((tm, N), lambda i: (i, 0)),
        out_shape=jax.ShapeDtypeStruct((M, N), F32),
        compiler_params=_cparams(("parallel",)), name=name)(h, w_bf, res, g, b)


def _lru_kernel(gate_ref, u_ref, h0_ref, cb0_ref, cw_ref, cbias_ref, wai_ref, ba_ref, bi_ref,
                lam_ref, o_ref, hl_ref, ext, hcar, *, Lc, L_valid, R):
    c = pl.program_id(1)

    @pl.when(c == 0)
    def _():
        ext[0:SUBLANES, :] = cb0_ref[0]
        hcar[...] = h0_ref[0][0:1, :]

    @pl.when(c > 0)
    def _():
        ext[0:SUBLANES, :] = ext[Lc:Lc + SUBLANES, :]

    ext[SUBLANES:SUBLANES + Lc, :] = u_ref[...]
    u = cbias_ref[...] + cw_ref[3:4, :] * ext[8:8 + Lc, :]
    for k in range(CONV_WIDTH - 1):
        u = u + cw_ref[k:k + 1, :] * ext[5 + k:5 + k + Lc, :]

    bw = R // LRU_BLOCKS
    za, zi = [], []
    for n in range(LRU_BLOCKS):
        z = jnp.dot(u[:, n * bw:(n + 1) * bw].astype(BF16), wai_ref[n], preferred_element_type=F32)
        za.append(z[:, :bw])
        zi.append(z[:, bw:])
    r = _sigmoid(jnp.concatenate(za, axis=1) + ba_ref[...])
    ig = _sigmoid(jnp.concatenate(zi, axis=1) + bi_ref[...])
    log_a = (-LRU_C) * r * _softplus(-lam_ref[...])
    a = jnp.exp(log_a)
    xin = jnp.sqrt(1.0 - jnp.exp(2.0 * log_a)) * ig * u

    rows = lax.broadcasted_iota(I32, (Lc, R), 0)
    if L_valid is not None:
        valid = (rows + c * Lc) < L_valid
        a = jnp.where(valid, a, 1.0)
        xin = jnp.where(valid, xin, 0.0)

    s = 1
    while s < Lc:
        a_sh = jnp.where(rows >= s, pltpu.roll(a, s, 0), 1.0)
        b_sh = jnp.where(rows >= s, pltpu.roll(xin, s, 0), 0.0)
        xin = a * b_sh + xin
        a = a * a_sh
        s *= 2
    h = a * hcar[...] + xin
    hcar[...] = h[Lc - 1:Lc, :]
    hl_ref[0] = jnp.broadcast_to(h[Lc - 1:Lc, :], (SUBLANES, R))
    o_ref[...] = h * jax.nn.gelu(gate_ref[...], approximate=True)


def _lru_seq(proj, prev_out, h0, cb0, p, *, row_off, B, Lp, Lc, L_valid, name):
    T_all, R2 = proj.shape
    R = R2 // 2
    nC = Lp // Lc
    off = row_off // Lc
    kern = functools.partial(_lru_kernel, Lc=Lc, L_valid=L_valid, R=R)
    row_map = lambda b, c: (off + b * nC + c, 0)
    const2 = lambda b, c: (0, 0)
    in_specs = [pl.BlockSpec((Lc, R), row_map),
                pl.BlockSpec((Lc, R), lambda b, c: (off + b * nC + c, 1)),
                pl.BlockSpec((1, SUBLANES, R), lambda b, c: (b, 0, 0)),
                pl.BlockSpec((1, SUBLANES, R), lambda b, c: (b, 0, 0)),
                pl.BlockSpec((SUBLANES, R), const2),
                pl.BlockSpec((1, R), const2),
                pl.BlockSpec((LRU_BLOCKS, R // LRU_BLOCKS, 2 * R // LRU_BLOCKS), lambda b, c: (0, 0, 0)),
                pl.BlockSpec((1, R), const2), pl.BlockSpec((1, R), const2), pl.BlockSpec((1, R), const2)]
    args = [proj, proj, h0, cb0, p["cw"], p["cb"], p["wai"], p["ba"], p["bi"], p["lam"]]
    aliases = {}
    if prev_out is not None:
        in_specs.append(pl.BlockSpec(memory_space=pl.ANY))
        args.append(prev_out)
        aliases = {len(args) - 1: 0}
        kern = _drop_extra_input(kern, n_in=10, n_extra=1)
    out, hl = pl.pallas_call(
        kern, grid=(B, nC), in_specs=in_specs,
        out_specs=[pl.BlockSpec((Lc, R), row_map), pl.BlockSpec((1, SUBLANES, R), lambda b, c: (b, 0, 0))],
        out_shape=[jax.ShapeDtypeStruct((T_all, R), F32), jax.ShapeDtypeStruct((B, SUBLANES, R), F32)],
        scratch_shapes=[pltpu.VMEM((Lc + SUBLANES, R), F32), pltpu.VMEM((1, R), F32)],
        input_output_aliases=aliases,
        compiler_params=_cparams(("parallel", "arbitrary")), name=name)(*args)
    return out, hl[:, 0, :]


def _drop_extra_input(kern, n_in, n_extra):
    def wrapped(*refs):
        return kern(*refs[:n_in], *refs[n_in + n_extra:])
    return wrapped


def _tri_inv(a_mat, C):
    ri = lax.broadcasted_iota(I32, (C, C), 0)
    ci = lax.broadcasted_iota(I32, (C, C), 1)
    x = jnp.where(ri == ci, 1.0, 0.0) - a_mat
    p = a_mat
    k = 2
    while k < C:
        p = _hdot(p, p)
        x = x + _hdot(x, p)
        k *= 2
    return x


def _gdn_kernel(qkv_ref, z_ref, ba_ref, s0_ref, cb0_ref, cw_ref, alog_ref, dtb_ref, nw_ref,
                o_ref, sfin_ref, ext, S, *, C, L_valid, nC):
    c = pl.program_id(1)
    QK = GDN_HK * GDN_DK

    @pl.when(c == 0)
    def _():
        ext[0:SUBLANES, :] = cb0_ref[0]
        S[...] = s0_ref[0]

    @pl.when(c > 0)
    def _():
        ext[0:SUBLANES, :] = ext[C:C + SUBLANES, :]

    ext[SUBLANES:SUBLANES + C, :] = qkv_ref[...]
    xc = cw_ref[3:4, :] * ext[8:8 + C, :]
    for k in range(CONV_WIDTH - 1):
        xc = xc + cw_ref[k:k + 1, :] * ext[5 + k:5 + k + C, :]
    act = xc * _sigmoid(xc)

    ba = ba_ref[...]
    beta = _sigmoid(ba)
    g = pltpu.roll(-jnp.exp(alog_ref[...]) * _softplus(ba + dtb_ref[...]), LANES - GDN_HV, 1)
    if L_valid is not None:
        rows = lax.broadcasted_iota(I32, (C, LANES), 0)
        valid = (rows + c * C) < L_valid
        g = jnp.where(valid, g, 0.0)
        beta = jnp.where(valid, beta, 0.0)
    gc = _row_scan_sum(g, C)
    gct = jnp.transpose(gc)
    e_gc = jnp.exp(gc)
    g_last = gc[C - 1:C, :]
    e_rest = jnp.exp(g_last - gc)
    e_last = jnp.exp(g_last)

    ri = lax.broadcasted_iota(I32, (C, C), 0)
    ci = lax.broadcasted_iota(I32, (C, C), 1)
    tril = ri >= ci
    strict = ri > ci

    qn, kn = [], []
    for hk in range(GDN_HK):
        qh = act[:, hk * GDN_DK:(hk + 1) * GDN_DK]
        kh = act[:, QK + hk * GDN_DK:QK + (hk + 1) * GDN_DK]
        qn.append(qh * lax.rsqrt(jnp.sum(qh * qh, -1, keepdims=True) + NORM_EPS) * (GDN_DK ** -0.5))
        kn.append(kh * lax.rsqrt(jnp.sum(kh * kh, -1, keepdims=True) + NORM_EPS))

    rep = GDN_HV // GDN_HK
    for h in range(GDN_HV):
        qh, kh = qn[h // rep], kn[h // rep]
        vh = act[:, 2 * QK + h * GDN_DV:2 * QK + (h + 1) * GDN_DV]
        bi = beta[:, h:h + 1]
        ei = e_gc[:, h:h + 1]
        decay = jnp.exp(jnp.where(tril, gc[:, h:h + 1] - gct[h:h + 1, :], -jnp.inf))
        a_mat = jnp.where(strict, bi * _nt(kh, kh) * decay, 0.0)
        t_mat = _tri_inv(a_mat, C)
        u_c = _bdot(t_mat, vh * bi)
        w_c = _bdot(t_mat, kh * (bi * ei))
        qk = _nt(qh, kh) * decay
        s_h = S[h]
        v_new = u_c - _bdot(w_c, s_h)
        o = _bdot(qh * ei, s_h) + _bdot(qk, v_new)
        S[h] = s_h * e_last[:, h:h + 1] + _tn(kh * e_rest[:, h:h + 1], v_new)
        o = o * lax.rsqrt(jnp.mean(o * o, -1, keepdims=True) + NORM_EPS) * nw_ref[...]
        zh = z_ref[:, h * GDN_DV:(h + 1) * GDN_DV]
        o_ref[:, h * GDN_DV:(h + 1) * GDN_DV] = o * (zh * _sigmoid(zh))

    @pl.when(c == nC - 1)
    def _():
        sfin_ref[0] = S[...]


def _gdn_seq(proj, ba, prev_out, s0, cb0, p, *, row_off, B, Lp, C, L_valid, name):
    T_all = proj.shape[0]
    QK, VD = GDN_HK * GDN_DK, GDN_HV * GDN_DV
    CD = 2 * QK + VD
    nC = Lp // C
    off = row_off // C
    kern = functools.partial(_gdn_kernel, C=C, L_valid=L_valid, nC=nC)
    row_map = lambda b, c: (off + b * nC + c, 0)
    const2 = lambda b, c: (0, 0)
    zblk = CD // VD
    in_specs = [pl.BlockSpec((C, CD), row_map),
                pl.BlockSpec((C, VD), lambda b, c: (off + b * nC + c, zblk)),
                pl.BlockSpec((C, LANES), row_map),
                pl.BlockSpec((1, GDN_HV, GDN_DK, GDN_DV), lambda b, c: (b, 0, 0, 0)),
                pl.BlockSpec((1, SUBLANES, CD), lambda b, c: (b, 0, 0)),
                pl.BlockSpec((SUBLANES, CD), const2),
                pl.BlockSpec((1, LANES), const2), pl.BlockSpec((1, LANES), const2),
                pl.BlockSpec((1, GDN_DV), const2)]
    args = [proj, proj, ba, s0, cb0, p["cw"], p["alog"], p["dtb"], p["nw"]]
    aliases = {}
    if prev_out is not None:
        in_specs.append(pl.BlockSpec(memory_space=pl.ANY))
        args.append(prev_out)
        aliases = {len(args) - 1: 0}
        kern = _drop_extra_input(kern, n_in=9, n_extra=1)
    out, sfin = pl.pallas_call(
        kern, grid=(B, nC), in_specs=in_specs,
        out_specs=[pl.BlockSpec((C, VD), row_map),
                   pl.BlockSpec((1, GDN_HV, GDN_DK, GDN_DV), lambda b, c: (b, 0, 0, 0))],
        out_shape=[jax.ShapeDtypeStruct((T_all, VD), F32),
                   jax.ShapeDtypeStruct((B, GDN_HV, GDN_DK, GDN_DV), F32)],
        scratch_shapes=[pltpu.VMEM((C + SUBLANES, CD), F32), pltpu.VMEM((GDN_HV, GDN_DK, GDN_DV), F32)],
        input_output_aliases=aliases,
        compiler_params=_cparams(("parallel", "arbitrary")), name=name)(*args)
    return out, sfin


def _rope(x, cos, ss):
    W = x.shape[1]
    reps = W // LANES
    cfull = jnp.concatenate([cos] * reps, axis=1)
    sfull = jnp.concatenate([ss] * reps, axis=1)
    lane = lax.broadcasted_iota(I32, x.shape, 1)
    first = (lane % SWA_HD) < (SWA_HD // 2)
    partner = jnp.where(first, pltpu.roll(x, W - SWA_HD // 2, 1), pltpu.roll(x, SWA_HD // 2, 1))
    return x * cfull + partner * sfull


def _swa_proj_kernel(x_ref, w_ref, cos_ref, ss_ref, o_ref, acc_ref, *, d, tm, n_rot):
    j = pl.program_id(1)
    acc = jnp.dot(x_ref[...].astype(BF16), w_ref[...], preferred_element_type=F32)

    def emit(val):
        if d == 1:
            o_ref[...] = val
            return
        n = tm // d
        for cb in range(val.shape[1] // LANES):
            acc_ref[cb] = val[:, cb * LANES:(cb + 1) * LANES]
        for cb in range(val.shape[1] // LANES):
            for r in range(d):
                o_ref[r * n:(r + 1) * n, cb * LANES:(cb + 1) * LANES] = acc_ref[cb, pl.ds(r, n, stride=d), :]

    @pl.when(j < n_rot)
    def _():
        emit(_rope(acc, cos_ref[...], ss_ref[...]))

    @pl.when(j >= n_rot)
    def _():
        emit(acc)


def _swa_proj(x, w_bf, cos, ss, *, d, tm, row_blk_off, n_row_blks, name):
    K = x.shape[1]
    N = w_bf.shape[1]
    tn = 512
    n_rot = (2 * N // 3) // tn
    kern = functools.partial(_swa_proj_kernel, d=d, tm=tm, n_rot=n_rot)
    return pl.pallas_call(
        kern, grid=(n_row_blks, N // tn),
        in_specs=[pl.BlockSpec((tm, K), lambda i, j: (i + row_blk_off, 0)),
                  pl.BlockSpec((K, tn), lambda i, j: (0, j)),
                  pl.BlockSpec((tm, LANES), lambda i, j: (i + row_blk_off, 0)),
                  pl.BlockSpec((tm, LANES), lambda i, j: (i + row_blk_off, 0))],
        out_specs=pl.BlockSpec((tm, tn), lambda i, j: (i, j)),
        out_shape=jax.ShapeDtypeStruct((n_row_blks * tm, N), F32),
        scratch_shapes=[pltpu.VMEM((tn // LANES, tm, LANES), F32)],
        compiler_params=_cparams(("parallel", "parallel")), name=name)(x, w_bf, cos, ss)


def _swa_attn_kernel(q_ref, kc_ref, kp_ref, vc_ref, vp_ref, o_ref, l_ref, *, nbt):
    t = pl.program_id(1)
    j = pl.program_id(2)
    has_prev = jnp.logical_or(t > 0, (j % nbt) > 0)
    QB = SWA_BLOCK
    qi = lax.broadcasted_iota(I32, (QB, QB), 0)
    ki = lax.broadcasted_iota(I32, (QB, QB), 1)
    mask_c = ki <= qi
    mask_p = jnp.logical_and(ki >= qi, has_prev)
    for h in range(SWA_HEADS):
        sl = slice(h * SWA_HD, (h + 1) * SWA_HD)
        qh = q_ref[:, sl] * (SWA_HD ** -0.5)
        sc = jnp.where(mask_c, _nt(qh, kc_ref[:, sl]), -jnp.inf)
        sp = jnp.where(mask_p, _nt(qh, kp_ref[:, sl]), -jnp.inf)
        m = jnp.maximum(jnp.max(sc, -1, keepdims=True), jnp.max(sp, -1, keepdims=True))
        pc = jnp.exp(sc - m)
        pp = jnp.exp(sp - m)
        l = jnp.sum(pc, -1, keepdims=True) + jnp.sum(pp, -1, keepdims=True)
        o = _bdot(pc, vc_ref[:, sl]) + _bdot(pp, vp_ref[:, sl])
        o_ref[:, sl] = o / l
        l_ref[:, sl] = jnp.broadcast_to(m + jnp.log(l), (QB, SWA_HD))


def _swa_attn(pg, *, B, S, d, name):
    W = SWA_HEADS * SWA_HD
    QB = SWA_BLOCK
    bpt = SWA_TILE // QB
    nbt = bpt // d
    nT = S // SWA_TILE
    nblk = S // QB

    def cur(col):
        return lambda b, t, j: (b * nblk + t * bpt + j, col)

    def prev(col):
        def f(b, t, j):
            first = (j % nbt) == 0
            pj = jnp.where(first, j + nbt - 1, j - 1)
            pt = jnp.where(first, jnp.maximum(t - 1, 0), t)
            return (b * nblk + pt * bpt + pj, col)
        return f

    blk = (QB, W)
    o, l = pl.pallas_call(
        functools.partial(_swa_attn_kernel, nbt=nbt), grid=(B, nT, bpt),
        in_specs=[pl.BlockSpec(blk, cur(0)), pl.BlockSpec(blk, cur(1)), pl.BlockSpec(blk, prev(1)),
                  pl.BlockSpec(blk, cur(2)), pl.BlockSpec(blk, prev(2))],
        out_specs=[pl.BlockSpec(blk, cur(0)), pl.BlockSpec(blk, cur(0))],
        out_shape=[jax.ShapeDtypeStruct((B * S, W), F32), jax.ShapeDtypeStruct((B * S, W), F32)],
        compiler_params=_cparams(("parallel", "parallel", "arbitrary")), name=name)(pg, pg, pg, pg, pg)
    return o, l


def _merge3(o0, l0, o1, l1, o2, l2):
    m = jnp.maximum(jnp.maximum(l0, l1), l2)
    w0 = jnp.exp(l0 - m)
    w1 = jnp.exp(l1 - m)
    w2 = jnp.exp(l2 - m)
    return (w0 * o0 + w1 * o1 + w2 * o2) / (w0 + w1 + w2)


def _swa_merge_kernel(o0, l0, o1, l1, o2, l2, out_ref, s1o, s1l, s2o, s2l, *, tm):
    for (src_o, src_l, dst_o, dst_l, d) in ((o1, l1, s1o, s1l, SWA_DILATIONS[1]),
                                            (o2, l2, s2o, s2l, SWA_DILATIONS[2])):
        n = tm // d
        for cb in range(dst_o.shape[0]):
            cs = slice(cb * LANES, (cb + 1) * LANES)
            for r in range(d):
                dst_o[cb, pl.ds(r, n, stride=d), :] = src_o[r * n:(r + 1) * n, cs]
                dst_l[cb, pl.ds(r, n, stride=d), :] = src_l[r * n:(r + 1) * n, cs]
    for cb in range(s1o.shape[0]):
        cs = slice(cb * LANES, (cb + 1) * LANES)
        out_ref[:, cs] = _merge3(o0[:, cs], l0[:, cs], s1o[cb], s1l[cb], s2o[cb], s2l[cb])


def _swa_merge(outs, lses, *, rows, total_rows, name):
    W = outs[0].shape[1]
    tm, tn = SWA_TILE, 256
    spec = pl.BlockSpec((tm, tn), lambda i, j: (i, j))
    args = [outs[0], lses[0], outs[1], lses[1], outs[2], lses[2]]
    return pl.pallas_call(
        functools.partial(_swa_merge_kernel, tm=tm), grid=(rows // tm, W // tn),
        in_specs=[spec] * 6, out_specs=spec,
        out_shape=jax.ShapeDtypeStruct((total_rows, W), F32),
        scratch_shapes=[pltpu.VMEM((tn // LANES, tm, LANES), F32)] * 4,
        compiler_params=_cparams(("parallel", "parallel")), name=name)(*args)


def _swa_sample_kernel(q_ref, k_ref, v_ref, ck_ref, cv_ref, o_ref, l_ref, *, d, window, Wb, res_blk):
    L = SAMPLE_PAD
    if res_blk is None:
        ck = ck_ref[...]
        cv = cv_ref[...]
        nk = Wb
        ccol = lax.broadcasted_iota(I32, (L, nk), 1)
    else:
        nk = (Wb // d) * res_blk
        ck = ck_ref[...].reshape(nk, ck_ref.shape[-1])
        cv = cv_ref[...].reshape(nk, cv_ref.shape[-1])
        flat = lax.broadcasted_iota(I32, (L, nk), 1)
        ccol = (flat // res_blk) * d + (flat % res_blk)
    lrow = lax.broadcasted_iota(I32, (L, nk), 0)
    diff = Wb + lrow - ccol
    mask_c = jnp.logical_and(diff % d == 0, diff <= window)
    ln = lax.broadcasted_iota(I32, (L, L), 0)
    lk = lax.broadcasted_iota(I32, (L, L), 1)
    dn = ln - lk
    mask_n = jnp.logical_and(jnp.logical_and(dn >= 0, dn % d == 0), dn <= window)
    for h in range(SWA_HEADS):
        sl = slice(h * SWA_HD, (h + 1) * SWA_HD)
        qh = q_ref[:, sl] * (SWA_HD ** -0.5)
        sc = jnp.where(mask_c, _nt(qh, ck[:, sl]), -jnp.inf)
        sn = jnp.where(mask_n, _nt(qh, k_ref[:, sl]), -jnp.inf)
        m = jnp.maximum(jnp.max(sc, -1, keepdims=True), jnp.max(sn, -1, keepdims=True))
        pc = jnp.exp(sc - m)
        pn = jnp.exp(sn - m)
        l = jnp.sum(pc, -1, keepdims=True) + jnp.sum(pn, -1, keepdims=True)
        o = _bdot(pc, cv[:, sl]) + _bdot(pn, v_ref[:, sl])
        o_ref[:, sl] = o / l
        l_ref[:, sl] = jnp.broadcast_to(m + jnp.log(l), (L, SWA_HD))


def _swa_sample_attn(ps, cache, *, d, window, name):
    Bd, Wb, W2 = cache.shape
    W = W2 // 2
    L = SAMPLE_PAD
    if d % SUBLANES == 0:
        res_blk = SUBLANES
        cview = cache.reshape(Bd, Wb // d, d, W2)
        ck_spec = pl.BlockSpec((None, Wb // d, res_blk, W), lambda b: (b, 0, 0, 0))
        cv_spec = pl.BlockSpec((None, Wb // d, res_blk, W), lambda b: (b, 0, 0, 1))
    else:
        res_blk = None
        cview = cache
        ck_spec = pl.BlockSpec((None, Wb, W), lambda b: (b, 0, 0))
        cv_spec = pl.BlockSpec((None, Wb, W), lambda b: (b, 0, 1))
    kern = functools.partial(_swa_sample_kernel, d=d, window=window, Wb=Wb, res_blk=res_blk)
    blk = (L, W)
    o, l = pl.pallas_call(
        kern, grid=(Bd,),
        in_specs=[pl.BlockSpec(blk, lambda b: (b, 0)), pl.BlockSpec(blk, lambda b: (b, 1)),
                  pl.BlockSpec(blk, lambda b: (b, 2)), ck_spec, cv_spec],
        out_specs=[pl.BlockSpec(blk, lambda b: (b, 0)), pl.BlockSpec(blk, lambda b: (b, 0))],
        out_shape=[jax.ShapeDtypeStruct((Bd * L, W), F32), jax.ShapeDtypeStruct((Bd * L, W), F32)],
        compiler_params=_cparams(("parallel",)), name=name)(ps, ps, ps, cview, cview)
    return o, l


def _merge_plain_kernel(o0, l0, o1, l1, o2, l2, out_ref):
    out_ref[...] = _merge3(o0[...], l0[...], o1[...], l1[...], o2[...], l2[...])


def _swa_merge_sample(outs, lses, prev_out, *, row_off, name):
    rows, W = outs[0].shape
    T_all = prev_out.shape[0]
    off = row_off // rows
    spec = pl.BlockSpec((rows, W), lambda i: (0, 0))
    args = [outs[0], lses[0], outs[1], lses[1], outs[2], lses[2], prev_out]
    kern = _drop_extra_input(_merge_plain_kernel, n_in=6, n_extra=1)
    return pl.pallas_call(
        kern, grid=(1,),
        in_specs=[spec] * 6 + [pl.BlockSpec(memory_space=pl.ANY)],
        out_specs=pl.BlockSpec((rows, W), lambda i: (off, 0)),
        out_shape=jax.ShapeDtypeStruct((T_all, W), F32),
        input_output_aliases={6: 0},
        compiler_params=_cparams(("arbitrary",)), name=name)(*args)


def _router_kernel(x_ref, w_ref, b_ref, info_ref, cnt_ref, car, *, tm):
    i = pl.program_id(0)

    @pl.when(i == 0)
    def _():
        car[...] = jnp.zeros_like(car)

    logits = _hdot(x_ref[...], w_ref[...]) + b_ref[...]
    m = jnp.max(logits, -1, keepdims=True)
    e = jnp.exp(logits - m)
    probs = e / jnp.sum(e, -1, keepdims=True)
    lane = lax.broadcasted_iota(I32, (tm, LANES), 1).astype(F32)
    BIG = 4.0 * LANES

    best = None
    for gi in range(N_EXPERT_GROUPS):
        lo = float(gi * EXPERTS_PER_GROUP)
        masked = jnp.where(jnp.logical_and(lane >= lo, lane < lo + EXPERTS_PER_GROUP), probs, -1.0)
        m1 = jnp.max(masked, -1, keepdims=True)
        i1 = jnp.min(jnp.where(masked == m1, lane, BIG), -1, keepdims=True)
        masked2 = jnp.where(lane == i1, -1.0, masked)
        m2 = jnp.max(masked2, -1, keepdims=True)
        i2 = jnp.min(jnp.where(masked2 == m2, lane, BIG), -1, keepdims=True)
        score = m1 + m2
        if best is None:
            best = (score, m1, i1, m2, i2)
        else:
            take = score > best[0]
            best = tuple(jnp.where(take, n, o) for n, o in zip((score, m1, i1, m2, i2), best))
    _, m1, i1, m2, i2 = best
    wsum = m1 + m2
    w1 = m1 / wsum
    w2 = m2 / wsum

    onehot = jnp.where(jnp.logical_or(lane == i1, lane == i2), 1.0, 0.0)
    ri = lax.broadcasted_iota(I32, (tm, tm), 0)
    ci = lax.broadcasted_iota(I32, (tm, tm), 1)
    lower = jnp.where(ri > ci, 1.0, 0.0)
    before = _bdot(lower, onehot) + car[...]
    r1 = jnp.sum(jnp.where(lane == i1, before, 0.0), -1, keepdims=True)
    r2 = jnp.sum(jnp.where(lane == i2, before, 0.0), -1, keepdims=True)
    car[...] = car[...] + jnp.sum(onehot, 0, keepdims=True)
    cnt_ref[...] = jnp.broadcast_to(car[...], cnt_ref.shape)

    info = jnp.zeros((tm, LANES), F32)
    for k, val in enumerate((i1, i2, w1, w2, r1, r2)):
        info = jnp.where(lane == float(k), val, info)
    info_ref[...] = info


def _router(x, rw_pad, rb_pad, name):
    T, D = x.shape
    tm = _pick(T, (256, 128, 64, 8))
    info, cnt = pl.pallas_call(
        functools.partial(_router_kernel, tm=tm), grid=(T // tm,),
        in_specs=[pl.BlockSpec((tm, D), lambda i: (i, 0)),
                  pl.BlockSpec((D, LANES), lambda i: (0, 0)),
                  pl.BlockSpec((1, LANES), lambda i: (0, 0))],
        out_specs=[pl.BlockSpec((tm, LANES), lambda i: (i, 0)),
                   pl.BlockSpec((SUBLANES, LANES), lambda i: (0, 0))],
        out_shape=[jax.ShapeDtypeStruct((T, LANES), F32), jax.ShapeDtypeStruct((SUBLANES, LANES), F32)],
        scratch_shapes=[pltpu.VMEM((1, LANES), F32)],
        compiler_params=_cparams(("arbitrary",)), name=name)(x, rw_pad, rb_pad)
    return info, cnt


def _row_copy(src, dst, s, t, sem):
    return pltpu.make_async_copy(src.at[pl.ds(s, 1), :], dst.at[pl.ds(t, 1), :], sem)


def _dispatch_kernel(dest_ref, x_hbm, xs_in, xs_out, sems, *, nblk):
    del xs_in
    i = pl.program_id(0)
    slot = i % 2

    def issue(k, carry):
        for a in range(2):
            _row_copy(x_hbm, xs_out, i * TOK_BLK + k, dest_ref[0, 0, 2 * k + a], sems.at[slot]).start()
        return carry

    lax.fori_loop(0, TOK_BLK, issue, 0)

    def drain(sl):
        def body(k, carry):
            for a in range(2):
                _row_copy(x_hbm, xs_out, 0, 0, sems.at[sl]).wait()
            return carry
        lax.fori_loop(0, TOK_BLK, body, 0)

    @pl.when(i > 0)
    def _():
        drain(1 - slot)

    @pl.when(i == nblk - 1)
    def _():
        drain(slot)


def _dispatch(x, dest, xs_zero, name):
    T, D = x.shape
    nblk = T // TOK_BLK
    dest3 = dest.reshape(nblk, 1, 2 * TOK_BLK)
    return pl.pallas_call(
        functools.partial(_dispatch_kernel, nblk=nblk), grid=(nblk,),
        in_specs=[pl.BlockSpec((1, 1, 2 * TOK_BLK), lambda i: (i, 0, 0), memory_space=pltpu.SMEM),
                  pl.BlockSpec(memory_space=pl.ANY), pl.BlockSpec(memory_space=pl.ANY)],
        out_specs=pl.BlockSpec(memory_space=pl.ANY),
        out_shape=jax.ShapeDtypeStruct(xs_zero.shape, F32),
        scratch_shapes=[pltpu.SemaphoreType.DMA((2,))],
        input_output_aliases={2: 0},
        compiler_params=_cparams(("arbitrary",)), name=name)(dest3, x, xs_zero)


def _expert_kernel(be_ref, nu_ref, x_ref, wg_ref, wu_ref, wd_ref, o_ref):
    i = pl.program_id(0)

    @pl.when(i < nu_ref[0])
    def _():
        x = x_ref[...].astype(BF16)
        g = jnp.dot(x, wg_ref[0].astype(BF16), preferred_element_type=F32)
        u = jnp.dot(x, wu_ref[0].astype(BF16), preferred_element_type=F32)
        hid = (g * _sigmoid(g)) * u
        o_ref[...] = jnp.dot(hid.astype(BF16), wd_ref[0].astype(BF16), preferred_element_type=F32)

    @pl.when(i >= nu_ref[0])
    def _():
        o_ref[...] = jnp.zeros_like(o_ref)


def _experts(xs, blk_e, n_used, w_gate, w_up, w_down, name):
    n_rows, D = xs.shape
    F = w_gate.shape[2]
    nb = n_rows // MOE_ROWS
    grid_spec = pltpu.PrefetchScalarGridSpec(
        num_scalar_prefetch=2, grid=(nb,),
        in_specs=[pl.BlockSpec((MOE_ROWS, D), lambda i, be, nu: (i, 0)),
                  pl.BlockSpec((1, D, F), lambda i, be, nu: (be[i], 0, 0)),
                  pl.BlockSpec((1, D, F), lambda i, be, nu: (be[i], 0, 0)),
                  pl.BlockSpec((1, F, D), lambda i, be, nu: (be[i], 0, 0))],
        out_specs=pl.BlockSpec((MOE_ROWS, D), lambda i, be, nu: (i, 0)))
    return pl.pallas_call(
        _expert_kernel, grid_spec=grid_spec,
        out_shape=jax.ShapeDtypeStruct((n_rows, D), F32),
        compiler_params=_cparams(("arbitrary",)), name=name)(blk_e, n_used, xs, w_gate, w_up, w_down)


def _combine_kernel(dcur_ref, dnext_ref, x_ref, info_ref, g_ref, b_ref, ys_hbm, o_ref, buf, sems, *, nblk):
    i = pl.program_id(0)
    slot = i % 2

    def issue(dref, sl):
        def body(k, carry):
            for a in range(2):
                _row_copy(ys_hbm, buf.at[sl, a], dref[0, 0, 2 * k + a], k, sems.at[sl]).start()
            return carry
        lax.fori_loop(0, TOK_BLK, body, 0)

    @pl.when(i == 0)
    def _():
        issue(dcur_ref, 0)

    @pl.when(i + 1 < nblk)
    def _():
        issue(dnext_ref, 1 - slot)

    def drain(k, carry):
        for a in range(2):
            _row_copy(ys_hbm, buf.at[slot, a], 0, 0, sems.at[slot]).wait()
        return carry

    lax.fori_loop(0, TOK_BLK, drain, 0)
    info = info_ref[...]
    y = info[:, 2:3] * buf[slot, 0] + info[:, 3:4] * buf[slot, 1]
    o_ref[...] = _layer_norm(DN_ALPHA * x_ref[...] + y, g_ref[...], b_ref[...])


def _combine(x, info, dest, ys, g, b, name):
    T, D = x.shape
    nblk = T // TOK_BLK
    dest3 = dest.reshape(nblk, 1, 2 * TOK_BLK)
    dspec = lambda f: pl.BlockSpec((1, 1, 2 * TOK_BLK), f, memory_space=pltpu.SMEM)
    return pl.pallas_call(
        functools.partial(_combine_kernel, nblk=nblk), grid=(nblk,),
        in_specs=[dspec(lambda i: (i, 0, 0)),
                  dspec(lambda i: (jnp.minimum(i + 1, nblk - 1), 0, 0)),
                  pl.BlockSpec((TOK_BLK, D), lambda i: (i, 0)),
                  pl.BlockSpec((TOK_BLK, LANES), lambda i: (i, 0)),
                  pl.BlockSpec((1, D), lambda i: (0, 0)), pl.BlockSpec((1, D), lambda i: (0, 0)),
                  pl.BlockSpec(memory_space=pl.ANY)],
        out_specs=pl.BlockSpec((TOK_BLK, D), lambda i: (i, 0)),
        out_shape=jax.ShapeDtypeStruct((T, D), F32),
        scratch_shapes=[pltpu.VMEM((2, 2, TOK_BLK, D), F32), pltpu.SemaphoreType.DMA((2,))],
        compiler_params=_cparams(("arbitrary",)), name=name)(dest3, dest3, x, info, g, b, ys)


def _moe_ln(x, rw_pad, rb_pad, w_gate, w_up, w_down, g, b, tag):
    T, D = x.shape
    info, cnt = _router(x, rw_pad, rb_pad, f"router_{tag}")
    e = info[:, 0:2].astype(I32)
    rank = info[:, 4:6].astype(I32)
    counts = cnt[0, :N_EXPERTS].astype(I32)
    padded = (counts + MOE_ROWS - 1) // MOE_ROWS * MOE_ROWS
    pend = jnp.cumsum(padded)
    pstart = pend - padded
    dest = (pstart[e] + rank).reshape(-1)
    n_blocks = (2 * T + N_EXPERTS * (MOE_ROWS - 1) + MOE_ROWS - 1) // MOE_ROWS
    blk_e = jnp.minimum(jnp.searchsorted(pend, jnp.arange(n_blocks, dtype=I32) * MOE_ROWS, side="right"),
                        N_EXPERTS - 1).astype(I32)
    n_used = (pend[-1:] // MOE_ROWS).astype(I32)
    xs = _dispatch(x, dest, jnp.zeros((n_blocks * MOE_ROWS, D), F32), f"dispatch_{tag}")
    ys = _experts(xs, blk_e, n_used, w_gate, w_up, w_down, f"experts_{tag}")
    return _combine(x, info, dest, ys, g, b, f"combine_{tag}")


def _pad_tail(buf):
    return jnp.pad(buf, ((0, 0), (SUBLANES - (CONV_WIDTH - 1), 0), (0, 0)))


def _lru_layer(x, T_p, B, S, Bd, Ld, state_h, state_conv, p, ln_g, ln_b, tag):
    R = p["lam"].shape[1]
    proj = _proj(x, p["w_in"], p["b_in"], f"lru_in_{tag}")
    zeros_h = jnp.zeros((B, SUBLANES, R), F32)
    hg, h_p = _lru_seq(proj, None, zeros_h, zeros_h, p, row_off=0, B=B, Lp=S,
                       Lc=_pick(S, (256, 128, 64, 8)), L_valid=None, name=f"lru_seq_p_{tag}")
    h0 = jnp.broadcast_to(state_h[:, None, :], (Bd, SUBLANES, R))
    hg, h_s = _lru_seq(proj, hg, h0, _pad_tail(state_conv), p, row_off=T_p, B=Bd, Lp=SAMPLE_PAD,
                       Lc=SAMPLE_PAD, L_valid=Ld, name=f"lru_seq_s_{tag}")
    x_new = _outproj_ln(hg, p["w_out"], x, ln_g, ln_b, f"lru_out_{tag}")
    u_p = proj[:T_p, R:].reshape(B, S, R)
    u_s = proj[T_p:, R:].reshape(Bd, SAMPLE_PAD, R)
    conv_p = u_p[:, S - (CONV_WIDTH - 1):]
    conv_s = jnp.concatenate([state_conv, u_s[:, :Ld]], axis=1)[:, -(CONV_WIDTH - 1):]
    return x_new, h_p, h_s, conv_p, conv_s


def _gdn_layer(x, T_p, B, S, Bd, Ld, state_s, state_conv, p, ln_g, ln_b, tag):
    QK, VD = GDN_HK * GDN_DK, GDN_HV * GDN_DV
    CD = 2 * QK + VD
    proj = _proj(x, p["w_main"], p["b_main"], f"gdn_in_{tag}")
    ba = _proj_hi(x, p["w_ba"], f"gdn_ba_{tag}")
    o, s_p = _gdn_seq(proj, ba, None, jnp.zeros((B, GDN_HV, GDN_DK, GDN_DV), F32),
                      jnp.zeros((B, SUBLANES, CD), F32), p, row_off=0, B=B, Lp=S,
                      C=math.gcd(S, GDN_CHUNK), L_valid=None, name=f"gdn_seq_p_{tag}")
    o, s_s = _gdn_seq(proj, ba, o, state_s, _pad_tail(state_conv), p, row_off=T_p, B=Bd, Lp=SAMPLE_PAD,
                      C=SAMPLE_PAD, L_valid=Ld, name=f"gdn_seq_s_{tag}")
    x_new = _outproj_ln(o, p["w_out"], x, ln_g, ln_b, f"gdn_out_{tag}")
    q_p = proj[:T_p, :CD].reshape(B, S, CD)
    q_s = proj[T_p:, :CD].reshape(Bd, SAMPLE_PAD, CD)
    conv_p = q_p[:, S - (CONV_WIDTH - 1):]
    conv_s = jnp.concatenate([state_conv, q_s[:, :Ld]], axis=1)[:, -(CONV_WIDTH - 1):]
    return x_new, s_p, s_s, conv_p, conv_s


def _unpermute_tail(pg, B, S, d, keep, col0, ncol):
    tiles = -(-keep // SWA_TILE)
    rows = tiles * SWA_TILE
    t = pg.reshape(B, S, -1)[:, S - rows:, col0:col0 + ncol]
    t = t.reshape(B, tiles, d, SWA_TILE // d, ncol).transpose(0, 1, 3, 2, 4).reshape(B, rows, ncol)
    return t[:, rows - keep:]


def _swa_layer(x, T_p, B, S, Bd, Ld, caches, p, rope_cos, rope_ss, ln_g, ln_b, tag):
    W = SWA_HEADS * SWA_HD
    T_all = x.shape[0]
    n_s = T_all - T_p
    outs, lses, outs_s, lses_s, new_p, new_s = [], [], [], [], [], []
    for gi in range(SWA_GROUPS):
        d, window = SWA_DILATIONS[gi], SWA_WINDOWS[gi]
        pg = _swa_proj(x, p["w_in"][gi], rope_cos, rope_ss, d=d, tm=SWA_TILE, row_blk_off=0,
                       n_row_blks=T_p // SWA_TILE, name=f"swa_in_p{gi}_{tag}")
        o, l = _swa_attn(pg, B=B, S=S, d=d, name=f"swa_attn_p{gi}_{tag}")
        outs.append(o)
        lses.append(l)
        keep = min(window, S)
        kv = _unpermute_tail(pg, B, S, d, keep, W, 2 * W)
        new_p.append(kv.reshape(B, keep, 2, SWA_HEADS, SWA_HD))

        ps = _swa_proj(x, p["w_in"][gi], rope_cos, rope_ss, d=1, tm=n_s, row_blk_off=T_p // n_s,
                       n_row_blks=1, name=f"swa_in_s{gi}_{tag}")
        cache = caches[gi]
        Wb = cache.shape[1]
        o_s, l_s = _swa_sample_attn(ps, cache.reshape(Bd, Wb, 2 * W), d=d, window=window,
                                    name=f"swa_attn_s{gi}_{tag}")
        outs_s.append(o_s)
        lses_s.append(l_s)
        new_s.append(ps.reshape(Bd, SAMPLE_PAD, 3 * W)[:, :Ld, W:].reshape(Bd, Ld, 2, SWA_HEADS, SWA_HD))
    full = _swa_merge(outs, lses, rows=T_p, total_rows=T_all, name=f"swa_merge_p_{tag}")
    full = _swa_merge_sample(outs_s, lses_s, full, row_off=T_p, name=f"swa_merge_s_{tag}")
    x_new = _outproj_ln(full, p["w_out"], x, ln_g, ln_b, f"swa_out_{tag}")
    return x_new, new_p, new_s


def _rope_tables(T_p, S, n_s):
    half = SWA_HD // 2
    inv = ROPE_THETA ** (-2.0 * jnp.arange(half, dtype=F32) / SWA_HD)
    pos = jnp.concatenate([jnp.arange(T_p, dtype=I32) % S,
                           PAST_LEN + jnp.arange(n_s, dtype=I32) % SAMPLE_PAD]).astype(F32)
    ang = pos[:, None] * inv[None, :]
    cos, sin = jnp.cos(ang), jnp.sin(ang)
    return jnp.concatenate([cos, cos, cos, cos], 1), jnp.concatenate([-sin, sin, -sin, sin], 1)


def kernel(x_prompt, x_sample, state_lru_h, state_lru_conv, state_gdn_s, state_gdn_conv,
           cache_swa_w128, cache_swa_w512, cache_swa_w2048,
           lru_w_in, lru_b_in, lru_conv_w, lru_conv_b, lru_w_a, lru_b_a, lru_w_i, lru_b_i,
           lru_lambda, lru_w_out,
           gdn_w_in, gdn_conv_w, gdn_a_log, gdn_dt_bias, gdn_norm_w, gdn_w_out,
           swa_w_in, swa_w_out,
           ln1_g, ln1_b, ln2_g, ln2_b,
           router_w, router_b, moe_w_gate, moe_w_up, moe_w_down):
    B, S, D = x_prompt.shape
    Bd, Ld, _ = x_sample.shape
    T_p = B * S
    n_s = Bd * SAMPLE_PAD
    assert S % SWA_TILE == 0 and Ld <= SAMPLE_PAD and T_p % n_s == 0
    xs_pad = jnp.pad(x_sample, ((0, 0), (0, SAMPLE_PAD - Ld), (0, 0)))
    x = jnp.concatenate([x_prompt.reshape(T_p, D), xs_pad.reshape(n_s, D)], axis=0)

    row = lambda v: v.reshape(1, -1)
    pad8 = lambda w: jnp.pad(w, ((0, SUBLANES - w.shape[0]), (0, 0)))
    rw_pad = jnp.pad(router_w, ((0, 0), (0, LANES - N_EXPERTS)))
    rb_pad = jnp.concatenate([router_b, jnp.full((LANES - N_EXPERTS,), -1e30, F32)]).reshape(1, LANES)
    rope_cos, rope_ss = _rope_tables(T_p, S, n_s)
    caches = (cache_swa_w128, cache_swa_w512, cache_swa_w2048)

    res = {k: [] for k in ("lru_h_p", "lru_h_s", "lru_c_p", "lru_c_s", "gdn_s_p", "gdn_s_s", "gdn_c_p",
                           "gdn_c_s")}
    swa_p = [[] for _ in range(SWA_GROUPS)]
    swa_s = [[] for _ in range(SWA_GROUPS)]
    for i in range(DEPTH):
        kind, j = LAYER_KIND[i], KIND_INDEX[i]
        g1, b1 = row(ln1_g[i]), row(ln1_b[i])
        if kind == 0:
            p = dict(w_in=lru_w_in[j].astype(BF16), b_in=row(lru_b_in[j]), cw=pad8(lru_conv_w[j]),
                     cb=row(lru_conv_b[j]),
                     wai=jnp.concatenate([lru_w_a[j], lru_w_i[j]], axis=-1).astype(BF16),
                     ba=row(lru_b_a[j]), bi=row(lru_b_i[j]), lam=row(lru_lambda[j]),
                     w_out=lru_w_out[j].astype(BF16))
            x, h_p, h_s, c_p, c_s = _lru_layer(x, T_p, B, S, Bd, Ld, state_lru_h[j], state_lru_conv[j], p,
                                               g1, b1, f"l{i}")
            res["lru_h_p"].append(h_p)
            res["lru_h_s"].append(h_s)
            res["lru_c_p"].append(c_p)
            res["lru_c_s"].append(c_s)
        elif kind == 1:
            QK, VD = GDN_HK * GDN_DK, GDN_HV * GDN_DV
            CD = 2 * QK + VD
            w_in = gdn_w_in[j]
            p = dict(w_main=w_in[:, :CD + VD].astype(BF16), b_main=jnp.zeros((1, CD + VD), F32),
                     w_ba=jnp.pad(w_in[:, CD + VD:], ((0, 0), (0, LANES - 2 * GDN_HV))),
                     cw=pad8(gdn_conv_w[j]),
                     alog=jnp.pad(gdn_a_log[j], (GDN_HV, LANES - 2 * GDN_HV)).reshape(1, LANES),
                     dtb=jnp.pad(gdn_dt_bias[j], (GDN_HV, LANES - 2 * GDN_HV)).reshape(1, LANES),
                     nw=row(gdn_norm_w[j]), w_out=gdn_w_out[j].astype(BF16))
            x, s_p, s_s, c_p, c_s = _gdn_layer(x, T_p, B, S, Bd, Ld, state_gdn_s[j], state_gdn_conv[j], p,
                                               g1, b1, f"l{i}")
            res["gdn_s_p"].append(s_p)
            res["gdn_s_s"].append(s_s)
            res["gdn_c_p"].append(c_p)
            res["gdn_c_s"].append(c_s)
        else:
            W = SWA_HEADS * SWA_HD
            w_in = swa_w_in[j].astype(BF16)
            p = dict(w_in=[w_in[:, gi * 3 * W:(gi + 1) * 3 * W] for gi in range(SWA_GROUPS)],
                     w_out=swa_w_out[j].astype(BF16))
            x, new_p, new_s = _swa_layer(x, T_p, B, S, Bd, Ld, tuple(c[j] for c in caches), p,
                                         rope_cos, rope_ss, g1, b1, f"l{i}")
            for gi in range(SWA_GROUPS):
                swa_p[gi].append(new_p[gi])
                swa_s[gi].append(new_s[gi])
        x = _moe_ln(x, rw_pad, rb_pad, moe_w_gate[i], moe_w_up[i], moe_w_down[i],
                    row(ln2_g[i]), row(ln2_b[i]), f"l{i}")

    y_prompt = x[:T_p].reshape(B, S, D)
    y_sample = x[T_p:].reshape(Bd, SAMPLE_PAD, D)[:, :Ld]
    st = lambda k: jnp.stack(res[k])
    return (y_prompt, y_sample, st("lru_h_p"), st("lru_h_s"), st("lru_c_p"), st("lru_c_s"),
            st("gdn_s_p"), st("gdn_s_s"), st("gdn_c_p"), st("gdn_c_s"),
            jnp.stack(swa_p[0]), jnp.stack(swa_s[0]), jnp.stack(swa_p[1]), jnp.stack(swa_s[1]),
            jnp.stack(swa_p[2]), jnp.stack(swa_s[2]))
```

```python
import functools
import math

import jax
import jax.numpy as jnp
from jax import lax
from jax.experimental import pallas as pl
from jax.experimental.pallas import tpu as pltpu

F32 = jnp.float32
BF16 = jnp.bfloat16
I32 = jnp.int32

DEPTH = 4
LAYER_KIND = (0, 1, 2, 0)
KIND_INDEX = (0, 0, 0, 1)
DN_ALPHA = (2 * DEPTH) ** 0.25
LN_EPS = 1e-5
NORM_EPS = 1e-6
CONV_WIDTH = 4
LRU_BLOCKS = 4
LRU_C = 8.0
GDN_HK = 8
GDN_HV = 16
GDN_DK = 128
GDN_DV = 128
GDN_CHUNK = 64
SWA_WINDOWS = (128, 512, 2048)
SWA_DILATIONS = (1, 4, 16)
SWA_GROUPS = 3
SWA_HEADS = 16
SWA_HD = 64
SWA_BLOCK = 128
ROPE_THETA = 10000.0
PAST_LEN = 8192
N_EXPERTS = 32
N_EXPERT_GROUPS = 4
EXPERTS_PER_GROUP = 8

LANES = 128
SUBLANES = 8
SAMPLE_PAD = 8
SWA_TILE = 2048
MOE_ROWS = 256
TOK_BLK = 128
VMEM_LIMIT = 56 * 1024 * 1024


def _cparams(sem):
    return pltpu.CompilerParams(dimension_semantics=sem, vmem_limit_bytes=VMEM_LIMIT)


def _pick(n, cands):
    for c in cands:
        if n % c == 0:
            return c
    raise ValueError(f"no tile for {n}")


def _bdot(a, b):
    return jnp.dot(a.astype(BF16), b.astype(BF16), preferred_element_type=F32)


def _nt(a, b):
    return lax.dot_general(a.astype(BF16), b.astype(BF16), (((1,), (1,)), ((), ())),
                           preferred_element_type=F32)


def _tn(a, b):
    return lax.dot_general(a.astype(BF16), b.astype(BF16), (((0,), (0,)), ((), ())),
                           preferred_element_type=F32)


def _split(a):
    hi = a.astype(BF16)
    return hi, (a - hi.astype(F32)).astype(BF16)


def _dot3(a, b):
    (ah, al), (bh, bl) = a, b
    return (jnp.dot(ah, bh, preferred_element_type=F32)
            + (jnp.dot(ah, bl, preferred_element_type=F32) + jnp.dot(al, bh, preferred_element_type=F32)))


def _sigmoid(x):
    return 1.0 / (1.0 + jnp.exp(-x))


def _softplus(x):
    return jnp.maximum(x, 0.0) + jnp.log(1.0 + jnp.exp(-jnp.abs(x)))


def _layer_norm(x, g, b):
    mu = jnp.mean(x, -1, keepdims=True)
    xc = x - mu
    var = jnp.mean(xc * xc, -1, keepdims=True)
    return xc * lax.rsqrt(var + LN_EPS) * g + b


def _row_scan_sum(x, n):
    rows = lax.broadcasted_iota(I32, x.shape, 0)
    s = 1
    while s < n:
        x = x + jnp.where(rows >= s, pltpu.roll(x, s, 0), 0.0)
        s *= 2
    return x


def _proj_kernel(x_ref, w_ref, b_ref, o_ref):
    o_ref[...] = jnp.dot(x_ref[...].astype(BF16), w_ref[...], preferred_element_type=F32) + b_ref[...]


def _proj(x, w_bf, b, name):
    M, K = x.shape
    N = w_bf.shape[1]
    tm = _pick(M, (1280, 1024, 512, 256, 128, 64, 8))
    tn = _pick(N, (1024, 896, 768, 512, 384, 256, 128))
    return pl.pallas_call(
        _proj_kernel, grid=(M // tm, N // tn),
        in_specs=[pl.BlockSpec((tm, K), lambda i, j: (i, 0)),
                  pl.BlockSpec((K, tn), lambda i, j: (0, j)),
                  pl.BlockSpec((1, tn), lambda i, j: (0, j))],
        out_specs=pl.BlockSpec((tm, tn), lambda i, j: (i, j)),
        out_shape=jax.ShapeDtypeStruct((M, N), F32),
        compiler_params=_cparams(("parallel", "parallel")), name=name)(x, w_bf, b)


def _outproj_ln_kernel(h_ref, w_ref, r_ref, g_ref, b_ref, o_ref):
    y = jnp.dot(h_ref[...].astype(BF16), w_ref[...], preferred_element_type=F32)
    o_ref[...] = _layer_norm(DN_ALPHA * r_ref[...] + y, g_ref[...], b_ref[...])


def _outproj_ln(h, w_bf, res, g, b, name):
    M, K = h.shape
    N = w_bf.shape[1]
    tm = _pick(M, (1280, 1024, 512, 256, 128, 64, 8))
    return pl.pallas_call(
        _outproj_ln_kernel, grid=(M // tm,),
        in_specs=[pl.BlockSpec((tm, K), lambda i: (i, 0)),
                  pl.BlockSpec((K, N), lambda i: (0, 0)),
                  pl.BlockSpec((tm, N), lambda i: (i, 0)),
                  pl.BlockSpec((1, N), lambda i: (0, 0)),
                  pl.BlockSpec((1, N), lambda i: (0, 0))],
        out_specs=pl.BlockSpec((tm, N), lambda i: (i, 0)),
        out_shape=jax.ShapeDtypeStruct((M, N), F32),
        compiler_params=_cparams(("parallel",)), name=name)(h, w_bf, res, g, b)


def _lru_kernel(gate_ref, u_ref, h0_ref, cb0_ref, cw_ref, cbias_ref, wai_ref, ba_ref, bi_ref,
                lam_ref, o_ref, hl_ref, ext, hcar, *, Lc, L_valid, R):
    c = pl.program_id(1)

    @pl.when(c == 0)
    def _():
        ext[0:SUBLANES, :] = cb0_ref[0]
        hcar[...] = h0_ref[0][0:1, :]

    @pl.when(c > 0)
    def _():
        ext[0:SUBLANES, :] = ext[Lc:Lc + SUBLANES, :]

    ext[SUBLANES:SUBLANES + Lc, :] = u_ref[...]
    u = cbias_ref[...] + cw_ref[3:4, :] * ext[8:8 + Lc, :]
    for k in range(CONV_WIDTH - 1):
        u = u + cw_ref[k:k + 1, :] * ext[5 + k:5 + k + Lc, :]

    bw = R // LRU_BLOCKS
    za, zi = [], []
    for n in range(LRU_BLOCKS):
        z = jnp.dot(u[:, n * bw:(n + 1) * bw].astype(BF16), wai_ref[n], preferred_element_type=F32)
        za.append(z[:, :bw])
        zi.append(z[:, bw:])
    r = _sigmoid(jnp.concatenate(za, axis=1) + ba_ref[...])
    ig = _sigmoid(jnp.concatenate(zi, axis=1) + bi_ref[...])
    log_a = (-LRU_C) * r * _softplus(-lam_ref[...])
    a = jnp.exp(log_a)
    th = jnp.tanh(log_a)
    xin = jnp.sqrt(-2.0 * th / (1.0 - th)) * ig * u

    rows = lax.broadcasted_iota(I32, (Lc, R), 0)
    if L_valid is not None:
        valid = (rows + c * Lc) < L_valid
        a = jnp.where(valid, a, 1.0)
        xin = jnp.where(valid, xin, 0.0)

    s = 1
    while s < Lc:
        a_sh = jnp.where(rows >= s, pltpu.roll(a, s, 0), 1.0)
        b_sh = jnp.where(rows >= s, pltpu.roll(xin, s, 0), 0.0)
        xin = a * b_sh + xin
        a = a * a_sh
        s *= 2
    h = a * hcar[...] + xin
    hcar[...] = h[Lc - 1:Lc, :]
    hl_ref[0] = jnp.broadcast_to(h[Lc - 1:Lc, :], (SUBLANES, R))
    o_ref[...] = h * jax.nn.gelu(gate_ref[...], approximate=True)


def _lru_seq(proj, prev_out, h0, cb0, p, *, row_off, B, Lp, Lc, L_valid, name):
    T_all, R2 = proj.shape
    R = R2 // 2
    nC = Lp // Lc
    off = row_off // Lc
    kern = functools.partial(_lru_kernel, Lc=Lc, L_valid=L_valid, R=R)
    row_map = lambda b, c: (off + b * nC + c, 0)
    const2 = lambda b, c: (0, 0)
    in_specs = [pl.BlockSpec((Lc, R), row_map),
                pl.BlockSpec((Lc, R), lambda b, c: (off + b * nC + c, 1)),
                pl.BlockSpec((1, SUBLANES, R), lambda b, c: (b, 0, 0)),
                pl.BlockSpec((1, SUBLANES, R), lambda b, c: (b, 0, 0)),
                pl.BlockSpec((SUBLANES, R), const2),
                pl.BlockSpec((1, R), const2),
                pl.BlockSpec((LRU_BLOCKS, R // LRU_BLOCKS, 2 * R // LRU_BLOCKS), lambda b, c: (0, 0, 0)),
                pl.BlockSpec((1, R), const2), pl.BlockSpec((1, R), const2), pl.BlockSpec((1, R), const2)]
    args = [proj, proj, h0, cb0, p["cw"], p["cb"], p["wai"], p["ba"], p["bi"], p["lam"]]
    aliases = {}
    if prev_out is not None:
        in_specs.append(pl.BlockSpec(memory_space=pl.ANY))
        args.append(prev_out)
        aliases = {len(args) - 1: 0}
        kern = _drop_extra_input(kern, n_in=10, n_extra=1)
    out, hl = pl.pallas_call(
        kern, grid=(B, nC), in_specs=in_specs,
        out_specs=[pl.BlockSpec((Lc, R), row_map), pl.BlockSpec((1, SUBLANES, R), lambda b, c: (b, 0, 0))],
        out_shape=[jax.ShapeDtypeStruct((T_all, R), F32), jax.ShapeDtypeStruct((B, SUBLANES, R), F32)],
        scratch_shapes=[pltpu.VMEM((Lc + SUBLANES, R), F32), pltpu.VMEM((1, R), F32)],
        input_output_aliases=aliases,
        compiler_params=_cparams(("parallel", "arbitrary")), name=name)(*args)
    return out, hl[:, 0, :]


def _drop_extra_input(kern, n_in, n_extra):
    def wrapped(*refs):
        return kern(*refs[:n_in], *refs[n_in + n_extra:])
    return wrapped


def _gdn_kernel(qkv_ref, z_ref, ba_ref, s0_ref, cb0_ref, cw_ref, alog_ref, dtb_ref, nw_ref,
                o_ref, sfin_ref, ext, S, *, C, L_valid, nC):
    c = pl.program_id(1)
    QK = GDN_HK * GDN_DK

    @pl.when(c == 0)
    def _():
        ext[0:SUBLANES, :] = cb0_ref[0]
        S[...] = s0_ref[0]

    @pl.when(c > 0)
    def _():
        ext[0:SUBLANES, :] = ext[C:C + SUBLANES, :]

    ext[SUBLANES:SUBLANES + C, :] = qkv_ref[...]
    xc = cw_ref[3:4, :] * ext[8:8 + C, :]
    for k in range(CONV_WIDTH - 1):
        xc = xc + cw_ref[k:k + 1, :] * ext[5 + k:5 + k + C, :]
    act = xc * _sigmoid(xc)

    ba = ba_ref[...]
    beta = _sigmoid(ba)
    g = pltpu.roll(-jnp.exp(alog_ref[...]) * _softplus(ba + dtb_ref[...]), LANES - GDN_HV, 1)
    if L_valid is not None:
        rows = lax.broadcasted_iota(I32, (C, LANES), 0)
        valid = (rows + c * C) < L_valid
        g = jnp.where(valid, g, 0.0)
        beta = jnp.where(valid, beta, 0.0)
    gc = _row_scan_sum(g, C)
    gct = jnp.transpose(gc)
    e_gc = jnp.exp(gc)
    g_last = gc[C - 1:C, :]
    e_rest = jnp.exp(g_last - gc)
    e_last = jnp.exp(g_last)

    ri = lax.broadcasted_iota(I32, (C, C), 0)
    ci = lax.broadcasted_iota(I32, (C, C), 1)
    tril = ri >= ci
    strict = ri > ci

    qn, kn = [], []
    for hk in range(GDN_HK):
        qh = act[:, hk * GDN_DK:(hk + 1) * GDN_DK]
        kh = act[:, QK + hk * GDN_DK:QK + (hk + 1) * GDN_DK]
        qn.append(qh * lax.rsqrt(jnp.sum(qh * qh, -1, keepdims=True) + NORM_EPS) * (GDN_DK ** -0.5))
        kn.append(kh * lax.rsqrt(jnp.sum(kh * kh, -1, keepdims=True) + NORM_EPS))

    rep = GDN_HV // GDN_HK
    heads = range(GDN_HV)
    qkk = [_nt(jnp.concatenate([qn[hk], kn[hk]], axis=0), kn[hk]) for hk in range(GDN_HK)]
    eye = jnp.where(ri == ci, 1.0, 0.0)
    decay, xs, ps, rhs = [], [], [], []
    for h in heads:
        kh = kn[h // rep]
        vh = act[:, 2 * QK + h * GDN_DV:2 * QK + (h + 1) * GDN_DV]
        bi = beta[:, h:h + 1]
        dec = jnp.exp(jnp.where(tril, gc[:, h:h + 1] - gct[h:h + 1, :], -jnp.inf))
        a_mat = jnp.where(strict, bi * qkk[h // rep][C:, :] * dec, 0.0)
        decay.append(dec)
        xs.append(eye - a_mat)
        ps.append(a_mat)
        rhs.append(jnp.concatenate([vh * bi, kh * (bi * e_gc[:, h:h + 1])], axis=1))
    p_sp = [_split(p) for p in ps]
    k = 2
    while k < C:
        ps = [_dot3(sp, sp) for sp in p_sp]
        p_sp = [_split(p) for p in ps]
        xs = [x + _dot3(_split(x), sp) for x, sp in zip(xs, p_sp)]
        k *= 2
    uw = [_bdot(x, r) for x, r in zip(xs, rhs)]
    ws = [_bdot(jnp.concatenate([uw[h][:, GDN_DV:], qn[h // rep] * e_gc[:, h:h + 1]], axis=0), S[h])
          for h in heads]
    v_new = [uw[h][:, :GDN_DV] - ws[h][:C, :] for h in heads]
    o_intra = [_bdot(qkk[h // rep][:C, :] * decay[h], v_new[h]) for h in heads]
    s_upd = [_tn(kn[h // rep] * e_rest[:, h:h + 1], v_new[h]) for h in heads]
    for h in heads:
        S[h] = S[h] * e_last[:, h:h + 1] + s_upd[h]
        o = ws[h][C:, :] + o_intra[h]
        o = o * lax.rsqrt(jnp.mean(o * o, -1, keepdims=True) + NORM_EPS) * nw_ref[...]
        zh = z_ref[:, h * GDN_DV:(h + 1) * GDN_DV]
        o_ref[:, h * GDN_DV:(h + 1) * GDN_DV] = o * (zh * _sigmoid(zh))

    @pl.when(c == nC - 1)
    def _():
        sfin_ref[0] = S[...]


def _gdn_seq(proj, prev_out, s0, cb0, p, *, row_off, B, Lp, C, L_valid, name):
    T_all = proj.shape[0]
    QK, VD = GDN_HK * GDN_DK, GDN_HV * GDN_DV
    CD = 2 * QK + VD
    nC = Lp // C
    off = row_off // C
    kern = functools.partial(_gdn_kernel, C=C, L_valid=L_valid, nC=nC)
    row_map = lambda b, c: (off + b * nC + c, 0)
    const2 = lambda b, c: (0, 0)
    zblk = CD // VD
    in_specs = [pl.BlockSpec((C, CD), row_map),
                pl.BlockSpec((C, VD), lambda b, c: (off + b * nC + c, zblk)),
                pl.BlockSpec((C, LANES), lambda b, c: (off + b * nC + c, (CD + VD) // LANES)),
                pl.BlockSpec((1, GDN_HV, GDN_DK, GDN_DV), lambda b, c: (b, 0, 0, 0)),
                pl.BlockSpec((1, SUBLANES, CD), lambda b, c: (b, 0, 0)),
                pl.BlockSpec((SUBLANES, CD), const2),
                pl.BlockSpec((1, LANES), const2), pl.BlockSpec((1, LANES), const2),
                pl.BlockSpec((1, GDN_DV), const2)]
    args = [proj, proj, proj, s0, cb0, p["cw"], p["alog"], p["dtb"], p["nw"]]
    aliases = {}
    if prev_out is not None:
        in_specs.append(pl.BlockSpec(memory_space=pl.ANY))
        args.append(prev_out)
        aliases = {len(args) - 1: 0}
        kern = _drop_extra_input(kern, n_in=9, n_extra=1)
    out, sfin = pl.pallas_call(
        kern, grid=(B, nC), in_specs=in_specs,
        out_specs=[pl.BlockSpec((C, VD), row_map),
                   pl.BlockSpec((1, GDN_HV, GDN_DK, GDN_DV), lambda b, c: (b, 0, 0, 0))],
        out_shape=[jax.ShapeDtypeStruct((T_all, VD), F32),
                   jax.ShapeDtypeStruct((B, GDN_HV, GDN_DK, GDN_DV), F32)],
        scratch_shapes=[pltpu.VMEM((C + SUBLANES, CD), F32), pltpu.VMEM((GDN_HV, GDN_DK, GDN_DV), F32)],
        input_output_aliases=aliases,
        compiler_params=_cparams(("parallel", "arbitrary")), name=name)(*args)
    return out, sfin


def _rope(x, cos, ss):
    W = x.shape[1]
    reps = W // LANES
    cfull = jnp.concatenate([cos] * reps, axis=1)
    sfull = jnp.concatenate([ss] * reps, axis=1)
    lane = lax.broadcasted_iota(I32, x.shape, 1)
    first = (lane % SWA_HD) < (SWA_HD // 2)
    partner = jnp.where(first, pltpu.roll(x, W - SWA_HD // 2, 1), pltpu.roll(x, SWA_HD // 2, 1))
    return x * cfull + partner * sfull


def _swa_proj_kernel(x_ref, w_ref, cos_ref, ss_ref, o_ref, acc_ref, *, d, tm, n_rot):
    j = pl.program_id(1)
    acc = jnp.dot(x_ref[...].astype(BF16), w_ref[...], preferred_element_type=F32)

    def emit(val):
        if d == 1:
            o_ref[...] = val
            return
        n = tm // d
        for cb in range(val.shape[1] // LANES):
            acc_ref[cb] = val[:, cb * LANES:(cb + 1) * LANES]
        for cb in range(val.shape[1] // LANES):
            for r in range(d):
                o_ref[r * n:(r + 1) * n, cb * LANES:(cb + 1) * LANES] = acc_ref[cb, pl.ds(r, n, stride=d), :]

    @pl.when(j < n_rot)
    def _():
        emit(_rope(acc, cos_ref[...], ss_ref[...]))

    @pl.when(j >= n_rot)
    def _():
        emit(acc)


def _swa_proj(x, w_bf, cos, ss, *, d, tm, row_blk_off, n_row_blks, name):
    K = x.shape[1]
    N = w_bf.shape[1]
    tn = 512
    n_rot = (2 * N // 3) // tn
    kern = functools.partial(_swa_proj_kernel, d=d, tm=tm, n_rot=n_rot)
    return pl.pallas_call(
        kern, grid=(n_row_blks, N // tn),
        in_specs=[pl.BlockSpec((tm, K), lambda i, j: (i + row_blk_off, 0)),
                  pl.BlockSpec((K, tn), lambda i, j: (0, j)),
                  pl.BlockSpec((tm, LANES), lambda i, j: (i + row_blk_off, 0)),
                  pl.BlockSpec((tm, LANES), lambda i, j: (i + row_blk_off, 0))],
        out_specs=pl.BlockSpec((tm, tn), lambda i, j: (i, j)),
        out_shape=jax.ShapeDtypeStruct((n_row_blks * tm, N), F32),
        scratch_shapes=[pltpu.VMEM((tn // LANES, tm, LANES), F32)],
        compiler_params=_cparams(("parallel", "parallel")), name=name)(x, w_bf, cos, ss)


def _swa_attn_kernel(q_ref, kc_ref, kp_ref, vc_ref, vp_ref, o_ref, l_ref, *, nbt):
    t = pl.program_id(1)
    j = pl.program_id(2)
    has_prev = jnp.logical_or(t > 0, (j % nbt) > 0)
    QB = SWA_BLOCK
    qi = lax.broadcasted_iota(I32, (QB, QB), 0)
    ki = lax.broadcasted_iota(I32, (QB, QB), 1)
    mask_c = ki <= qi
    mask_p = jnp.logical_and(ki >= qi, has_prev)
    sls = [slice(h * SWA_HD, (h + 1) * SWA_HD) for h in range(SWA_HEADS)]
    qs = [(q_ref[:, sl] * (SWA_HD ** -0.5)).astype(BF16) for sl in sls]
    scs = [_nt(q, kc_ref[:, sl]) for q, sl in zip(qs, sls)]
    sps = [_nt(q, kp_ref[:, sl]) for q, sl in zip(qs, sls)]
    pcs, pps, ms, ls = [], [], [], []
    for sc, sp in zip(scs, sps):
        sc = jnp.where(mask_c, sc, -jnp.inf)
        sp = jnp.where(mask_p, sp, -jnp.inf)
        m = jnp.maximum(jnp.max(sc, -1, keepdims=True), jnp.max(sp, -1, keepdims=True))
        pc = jnp.exp(sc - m)
        pp = jnp.exp(sp - m)
        l = jnp.sum(pc, -1, keepdims=True) + jnp.sum(pp, -1, keepdims=True)
        inv_l = 1.0 / l
        ls.append(l)
        ms.append(m)
        pcs.append((pc * inv_l).astype(BF16))
        pps.append((pp * inv_l).astype(BF16))
    ocs = [_bdot(pc, vc_ref[:, sl]) for pc, sl in zip(pcs, sls)]
    ops = [_bdot(pp, vp_ref[:, sl]) for pp, sl in zip(pps, sls)]
    for h, sl in enumerate(sls):
        o_ref[:, sl] = ocs[h] + ops[h]
        l_ref[:, sl] = jnp.broadcast_to(ms[h] + jnp.log(ls[h]), (QB, SWA_HD))


def _swa_attn(pg, *, B, S, d, name):
    W = SWA_HEADS * SWA_HD
    QB = SWA_BLOCK
    bpt = SWA_TILE // QB
    nbt = bpt // d
    nT = S // SWA_TILE
    nblk = S // QB

    def cur(col):
        return lambda b, t, j: (b * nblk + t * bpt + j, col)

    def prev(col):
        def f(b, t, j):
            first = (j % nbt) == 0
            pj = jnp.where(first, j + nbt - 1, j - 1)
            pt = jnp.where(first, jnp.maximum(t - 1, 0), t)
            return (b * nblk + pt * bpt + pj, col)
        return f

    blk = (QB, W)
    o, l = pl.pallas_call(
        functools.partial(_swa_attn_kernel, nbt=nbt), grid=(B, nT, bpt),
        in_specs=[pl.BlockSpec(blk, cur(0)), pl.BlockSpec(blk, cur(1)), pl.BlockSpec(blk, prev(1)),
                  pl.BlockSpec(blk, cur(2)), pl.BlockSpec(blk, prev(2))],
        out_specs=[pl.BlockSpec(blk, cur(0)), pl.BlockSpec(blk, cur(0))],
        out_shape=[jax.ShapeDtypeStruct((B * S, W), F32), jax.ShapeDtypeStruct((B * S, W), F32)],
        compiler_params=_cparams(("parallel", "parallel", "arbitrary")), name=name)(pg, pg, pg, pg, pg)
    return o, l


def _merge3(o0, l0, o1, l1, o2, l2):
    m = jnp.maximum(jnp.maximum(l0, l1), l2)
    w0 = jnp.exp(l0 - m)
    w1 = jnp.exp(l1 - m)
    w2 = jnp.exp(l2 - m)
    return (w0 * o0 + w1 * o1 + w2 * o2) / (w0 + w1 + w2)


def _swa_merge_kernel(o0, l0, o1, l1, o2, l2, out_ref, s1o, s1l, s2o, s2l, *, tm):
    for (src_o, src_l, dst_o, dst_l, d) in ((o1, l1, s1o, s1l, SWA_DILATIONS[1]),
                                            (o2, l2, s2o, s2l, SWA_DILATIONS[2])):
        n = tm // d
        for cb in range(dst_o.shape[0]):
            cs = slice(cb * LANES, (cb + 1) * LANES)
            for r in range(d):
                dst_o[cb, pl.ds(r, n, stride=d), :] = src_o[r * n:(r + 1) * n, cs]
                dst_l[cb, pl.ds(r, n, stride=d), :] = src_l[r * n:(r + 1) * n, cs]
    for cb in range(s1o.shape[0]):
        cs = slice(cb * LANES, (cb + 1) * LANES)
        out_ref[:, cs] = _merge3(o0[:, cs], l0[:, cs], s1o[cb], s1l[cb], s2o[cb], s2l[cb])


def _swa_merge(outs, lses, *, rows, total_rows, name):
    W = outs[0].shape[1]
    tm, tn = SWA_TILE, 256
    spec = pl.BlockSpec((tm, tn), lambda i, j: (i, j))
    args = [outs[0], lses[0], outs[1], lses[1], outs[2], lses[2]]
    return pl.pallas_call(
        functools.partial(_swa_merge_kernel, tm=tm), grid=(rows // tm, W // tn),
        in_specs=[spec] * 6, out_specs=spec,
        out_shape=jax.ShapeDtypeStruct((total_rows, W), F32),
        scratch_shapes=[pltpu.VMEM((tn // LANES, tm, LANES), F32)] * 4,
        compiler_params=_cparams(("parallel", "parallel")), name=name)(*args)


def _swa_sample_kernel(q_ref, k_ref, v_ref, ck_ref, cv_ref, o_ref, l_ref, *, d, window, Wb):
    L = SAMPLE_PAD
    nres, per = ck_ref.shape[0], ck_ref.shape[1]
    nk = nres * per
    ck = ck_ref[...].reshape(nk, ck_ref.shape[-1])
    cv = cv_ref[...].reshape(nk, cv_ref.shape[-1])
    flat = lax.broadcasted_iota(I32, (L, nk), 1)
    ccol = (flat % per) * d + flat // per
    lrow = lax.broadcasted_iota(I32, (L, nk), 0)
    diff = Wb + lrow - ccol
    mask_c = jnp.logical_and(diff % d == 0, diff <= window)
    ln = lax.broadcasted_iota(I32, (L, L), 0)
    lk = lax.broadcasted_iota(I32, (L, L), 1)
    dn = ln - lk
    mask_n = jnp.logical_and(jnp.logical_and(dn >= 0, dn % d == 0), dn <= window)
    sls = [slice(h * SWA_HD, (h + 1) * SWA_HD) for h in range(SWA_HEADS)]
    qs = [(q_ref[:, sl] * (SWA_HD ** -0.5)).astype(BF16) for sl in sls]
    scs = [_nt(q, ck[:, sl]) for q, sl in zip(qs, sls)]
    sns = [_nt(q, k_ref[:, sl]) for q, sl in zip(qs, sls)]
    pcs, pns, ms, ls = [], [], [], []
    for sc, sn in zip(scs, sns):
        sc = jnp.where(mask_c, sc, -jnp.inf)
        sn = jnp.where(mask_n, sn, -jnp.inf)
        m = jnp.maximum(jnp.max(sc, -1, keepdims=True), jnp.max(sn, -1, keepdims=True))
        pc = jnp.exp(sc - m)
        pn = jnp.exp(sn - m)
        l = jnp.sum(pc, -1, keepdims=True) + jnp.sum(pn, -1, keepdims=True)
        inv_l = 1.0 / l
        ls.append(l)
        ms.append(m)
        pcs.append((pc * inv_l).astype(BF16))
        pns.append((pn * inv_l).astype(BF16))
    ocs = [_bdot(pc, cv[:, sl]) for pc, sl in zip(pcs, sls)]
    ons = [_bdot(pn, v_ref[:, sl]) for pn, sl in zip(pns, sls)]
    for h, sl in enumerate(sls):
        o_ref[:, sl] = ocs[h] + ons[h]
        l_ref[:, sl] = jnp.broadcast_to(ms[h] + jnp.log(ls[h]), (L, SWA_HD))


def _swa_sample_attn(ps, cache, *, d, window, name):
    Bd, nres, per, W2 = cache.shape
    W = W2 // 2
    Wb = per * d
    L = SAMPLE_PAD
    cview = cache
    ck_spec = pl.BlockSpec((None, nres, per, W), lambda b: (b, 0, 0, 0))
    cv_spec = pl.BlockSpec((None, nres, per, W), lambda b: (b, 0, 0, 1))
    kern = functools.partial(_swa_sample_kernel, d=d, window=window, Wb=Wb)
    blk = (L, W)
    o, l = pl.pallas_call(
        kern, grid=(Bd,),
        in_specs=[pl.BlockSpec(blk, lambda b: (b, 0)), pl.BlockSpec(blk, lambda b: (b, 1)),
                  pl.BlockSpec(blk, lambda b: (b, 2)), ck_spec, cv_spec],
        out_specs=[pl.BlockSpec(blk, lambda b: (b, 0)), pl.BlockSpec(blk, lambda b: (b, 0))],
        out_shape=[jax.ShapeDtypeStruct((Bd * L, W), F32), jax.ShapeDtypeStruct((Bd * L, W), F32)],
        compiler_params=_cparams(("parallel",)), name=name)(ps, ps, ps, cview, cview)
    return o, l


def _merge_plain_kernel(o0, l0, o1, l1, o2, l2, out_ref):
    out_ref[...] = _merge3(o0[...], l0[...], o1[...], l1[...], o2[...], l2[...])


def _swa_merge_sample(outs, lses, prev_out, *, row_off, name):
    rows, W = outs[0].shape
    T_all = prev_out.shape[0]
    off = row_off // rows
    spec = pl.BlockSpec((rows, W), lambda i: (0, 0))
    args = [outs[0], lses[0], outs[1], lses[1], outs[2], lses[2], prev_out]
    kern = _drop_extra_input(_merge_plain_kernel, n_in=6, n_extra=1)
    return pl.pallas_call(
        kern, grid=(1,),
        in_specs=[spec] * 6 + [pl.BlockSpec(memory_space=pl.ANY)],
        out_specs=pl.BlockSpec((rows, W), lambda i: (off, 0)),
        out_shape=jax.ShapeDtypeStruct((T_all, W), F32),
        input_output_aliases={6: 0},
        compiler_params=_cparams(("arbitrary",)), name=name)(*args)


def _router_kernel(x_ref, w_ref, b_ref, info_ref, cnt_ref, car, *, tm):
    i = pl.program_id(0)

    @pl.when(i == 0)
    def _():
        car[...] = jnp.zeros_like(car)

    logits = jnp.dot(x_ref[...], w_ref[...], precision=lax.Precision.HIGHEST,
                     preferred_element_type=F32) + b_ref[...]
    m = jnp.max(logits, -1, keepdims=True)
    e = jnp.exp(logits - m)
    probs = e / jnp.sum(e, -1, keepdims=True)
    lane = lax.broadcasted_iota(I32, (tm, LANES), 1).astype(F32)
    BIG = 4.0 * LANES

    best = None
    for gi in range(N_EXPERT_GROUPS):
        lo = float(gi * EXPERTS_PER_GROUP)
        masked = jnp.where(jnp.logical_and(lane >= lo, lane < lo + EXPERTS_PER_GROUP), probs, -1.0)
        m1 = jnp.max(masked, -1, keepdims=True)
        i1 = jnp.min(jnp.where(masked == m1, lane, BIG), -1, keepdims=True)
        masked2 = jnp.where(lane == i1, -1.0, masked)
        m2 = jnp.max(masked2, -1, keepdims=True)
        i2 = jnp.min(jnp.where(masked2 == m2, lane, BIG), -1, keepdims=True)
        score = m1 + m2
        if best is None:
            best = (score, m1, i1, m2, i2)
        else:
            take = score > best[0]
            best = tuple(jnp.where(take, n, o) for n, o in zip((score, m1, i1, m2, i2), best))
    _, m1, i1, m2, i2 = best
    wsum = m1 + m2
    w1 = m1 / wsum
    w2 = m2 / wsum

    onehot = jnp.where(jnp.logical_or(lane == i1, lane == i2), 1.0, 0.0)
    ri = lax.broadcasted_iota(I32, (tm, tm), 0)
    ci = lax.broadcasted_iota(I32, (tm, tm), 1)
    lower = jnp.where(ri > ci, 1.0, 0.0)
    before = _bdot(lower, onehot) + car[...]
    r1 = jnp.sum(jnp.where(lane == i1, before, 0.0), -1, keepdims=True)
    r2 = jnp.sum(jnp.where(lane == i2, before, 0.0), -1, keepdims=True)
    car[...] = car[...] + jnp.sum(onehot, 0, keepdims=True)
    cnt_ref[...] = jnp.broadcast_to(car[...], cnt_ref.shape)

    info = jnp.zeros((tm, LANES), F32)
    for k, val in enumerate((i1, i2, w1, w2, r1, r2)):
        info = jnp.where(lane == float(k), val, info)
    info_ref[...] = info


def _router(x, rw_pad, rb_pad, name):
    T, D = x.shape
    tm = _pick(T, (256, 128, 64, 8))
    info, cnt = pl.pallas_call(
        functools.partial(_router_kernel, tm=tm), grid=(T // tm,),
        in_specs=[pl.BlockSpec((tm, D), lambda i: (i, 0)),
                  pl.BlockSpec((D, LANES), lambda i: (0, 0)),
                  pl.BlockSpec((1, LANES), lambda i: (0, 0))],
        out_specs=[pl.BlockSpec((tm, LANES), lambda i: (i, 0)),
                   pl.BlockSpec((SUBLANES, LANES), lambda i: (0, 0))],
        out_shape=[jax.ShapeDtypeStruct((T, LANES), F32), jax.ShapeDtypeStruct((SUBLANES, LANES), F32)],
        scratch_shapes=[pltpu.VMEM((1, LANES), F32)],
        compiler_params=_cparams(("arbitrary",)), name=name)(x, rw_pad, rb_pad)
    return info, cnt


def _row_copy(src, dst, s, t, sem):
    return pltpu.make_async_copy(src.at[pl.ds(s, 1), :], dst.at[pl.ds(t, 1), :], sem)


DISPATCH_SLOTS = 3
ISSUE_UNROLL = 8


def _dispatch_kernel(dest_ref, x_hbm, xs_in, xs_out, buf, lsem, ssem, *, nblk):
    del xs_in
    i = pl.program_id(0)

    def load(blk):
        sl = blk % DISPATCH_SLOTS
        return pltpu.make_async_copy(x_hbm.at[pl.ds(blk * TOK_BLK, TOK_BLK), :], buf.at[sl], lsem.at[sl])

    def drain(blk):
        sl = blk % DISPATCH_SLOTS
        for a in range(2):
            pltpu.make_async_copy(buf.at[sl], xs_out.at[pl.ds(0, TOK_BLK), :], ssem.at[sl]).wait()

    @pl.when(i == 0)
    def _():
        load(0).start()
        if nblk > 1:
            load(1).start()

    slot = i % DISPATCH_SLOTS
    load(i).wait()

    def issue(k, carry):
        for a in range(2):
            _row_copy(buf.at[slot], xs_out, k, dest_ref[0, 0, 2 * k + a], ssem.at[slot]).start()
        return carry

    lax.fori_loop(0, TOK_BLK, issue, 0, unroll=ISSUE_UNROLL)

    @pl.when(i > 0)
    def _():
        drain(i - 1)

    @pl.when(i + 2 < nblk)
    def _():
        load(i + 2).start()

    @pl.when(i == nblk - 1)
    def _():
        drain(i)


def _dispatch(x, dest, xs_zero, name):
    T, D = x.shape
    nblk = T // TOK_BLK
    dest3 = dest.reshape(nblk, 1, 2 * TOK_BLK)
    return pl.pallas_call(
        functools.partial(_dispatch_kernel, nblk=nblk), grid=(nblk,),
        in_specs=[pl.BlockSpec((1, 1, 2 * TOK_BLK), lambda i: (i, 0, 0), memory_space=pltpu.SMEM),
                  pl.BlockSpec(memory_space=pl.ANY), pl.BlockSpec(memory_space=pl.ANY)],
        out_specs=pl.BlockSpec(memory_space=pl.ANY),
        out_shape=jax.ShapeDtypeStruct(xs_zero.shape, F32),
        scratch_shapes=[pltpu.VMEM((DISPATCH_SLOTS, TOK_BLK, D), F32),
                        pltpu.SemaphoreType.DMA((DISPATCH_SLOTS,)),
                        pltpu.SemaphoreType.DMA((DISPATCH_SLOTS,))],
        input_output_aliases={2: 0},
        compiler_params=_cparams(("arbitrary",)), name=name)(dest3, x, xs_zero)


def _expert_kernel(be_ref, nu_ref, x_ref, wg_ref, wu_ref, wd_ref, o_ref):
    i = pl.program_id(0)

    @pl.when(i < nu_ref[0])
    def _():
        x = x_ref[...].astype(BF16)
        g = jnp.dot(x, wg_ref[0].astype(BF16), preferred_element_type=F32)
        u = jnp.dot(x, wu_ref[0].astype(BF16), preferred_element_type=F32)
        hid = (g * _sigmoid(g)) * u
        o_ref[...] = jnp.dot(hid.astype(BF16), wd_ref[0].astype(BF16), preferred_element_type=F32)

    @pl.when(i >= nu_ref[0])
    def _():
        o_ref[...] = jnp.zeros_like(o_ref)


def _experts(xs, blk_e, n_used, w_gate, w_up, w_down, layer, name):
    n_rows, D = xs.shape
    F = w_gate.shape[3]
    nb = n_rows // MOE_ROWS
    grid_spec = pltpu.PrefetchScalarGridSpec(
        num_scalar_prefetch=2, grid=(nb,),
        in_specs=[pl.BlockSpec((MOE_ROWS, D), lambda i, be, nu: (i, 0)),
                  pl.BlockSpec((None, 1, D, F), lambda i, be, nu: (layer, be[i], 0, 0)),
                  pl.BlockSpec((None, 1, D, F), lambda i, be, nu: (layer, be[i], 0, 0)),
                  pl.BlockSpec((None, 1, F, D), lambda i, be, nu: (layer, be[i], 0, 0))],
        out_specs=pl.BlockSpec((MOE_ROWS, D), lambda i, be, nu: (i, 0)))
    return pl.pallas_call(
        _expert_kernel, grid_spec=grid_spec,
        out_shape=jax.ShapeDtypeStruct((n_rows, D), F32),
        compiler_params=_cparams(("arbitrary",)), name=name)(blk_e, n_used, xs, w_gate, w_up, w_down)


def _combine_kernel(dcur_ref, dnext_ref, x_ref, info_ref, g_ref, b_ref, ys_hbm, o_ref, buf, sems, *, nblk):
    i = pl.program_id(0)
    slot = i % 2

    def issue(dref, sl):
        def body(k, carry):
            for a in range(2):
                _row_copy(ys_hbm, buf.at[sl, a], dref[0, 0, 2 * k + a], k, sems.at[sl]).start()
            return carry
        lax.fori_loop(0, TOK_BLK, body, 0, unroll=ISSUE_UNROLL)

    @pl.when(i == 0)
    def _():
        issue(dcur_ref, 0)

    @pl.when(i + 1 < nblk)
    def _():
        issue(dnext_ref, 1 - slot)

    for a in range(2):
        pltpu.make_async_copy(ys_hbm.at[pl.ds(0, TOK_BLK), :], buf.at[slot, a], sems.at[slot]).wait()
    info = info_ref[...]
    y = info[:, 2:3] * buf[slot, 0] + info[:, 3:4] * buf[slot, 1]
    o_ref[...] = _layer_norm(DN_ALPHA * x_ref[...] + y, g_ref[...], b_ref[...])


def _combine(x, info, dest, ys, g, b, name):
    T, D = x.shape
    nblk = T // TOK_BLK
    dest3 = dest.reshape(nblk, 1, 2 * TOK_BLK)
    dspec = lambda f: pl.BlockSpec((1, 1, 2 * TOK_BLK), f, memory_space=pltpu.SMEM)
    return pl.pallas_call(
        functools.partial(_combine_kernel, nblk=nblk), grid=(nblk,),
        in_specs=[dspec(lambda i: (i, 0, 0)),
                  dspec(lambda i: (jnp.minimum(i + 1, nblk - 1), 0, 0)),
                  pl.BlockSpec((TOK_BLK, D), lambda i: (i, 0)),
                  pl.BlockSpec((TOK_BLK, LANES), lambda i: (i, 0)),
                  pl.BlockSpec((1, D), lambda i: (0, 0)), pl.BlockSpec((1, D), lambda i: (0, 0)),
                  pl.BlockSpec(memory_space=pl.ANY)],
        out_specs=pl.BlockSpec((TOK_BLK, D), lambda i: (i, 0)),
        out_shape=jax.ShapeDtypeStruct((T, D), F32),
        scratch_shapes=[pltpu.VMEM((2, 2, TOK_BLK, D), F32), pltpu.SemaphoreType.DMA((2,))],
        compiler_params=_cparams(("arbitrary",)), name=name)(dest3, dest3, x, info, g, b, ys)


def _moe_ln(x, rw_pad, rb_pad, w_gate, w_up, w_down, layer, g, b, tag):
    T, D = x.shape
    info, cnt = _router(x, rw_pad, rb_pad, f"router_{tag}")
    e = info[:, 0:2].astype(I32)
    rank = info[:, 4:6].astype(I32)
    counts = cnt[0, :N_EXPERTS].astype(I32)
    padded = (counts + MOE_ROWS - 1) // MOE_ROWS * MOE_ROWS
    pend = jnp.cumsum(padded)
    pstart = pend - padded
    dest = (pstart[e] + rank).reshape(-1)
    n_blocks = (2 * T + N_EXPERTS * (MOE_ROWS - 1) + MOE_ROWS - 1) // MOE_ROWS
    blk_start = jnp.arange(n_blocks, dtype=I32) * MOE_ROWS
    blk_e = jnp.minimum(jnp.sum((pend[None, :] <= blk_start[:, None]).astype(I32), axis=1), N_EXPERTS - 1)
    n_used = (pend[-1:] // MOE_ROWS).astype(I32)
    xs = _dispatch(x, dest, jnp.zeros((n_blocks * MOE_ROWS, D), F32), f"dispatch_{tag}")
    ys = _experts(xs, blk_e, n_used, w_gate, w_up, w_down, layer, f"experts_{tag}")
    return _combine(x, info, dest, ys, g, b, f"combine_{tag}")


def _tail_rows(a2d, B, S, n, col0, ncol):
    return jnp.stack([a2d[b * S + S - n:b * S + S, col0:col0 + ncol] for b in range(B)])


def _pad_tail(buf):
    return jnp.pad(buf, ((0, 0), (SUBLANES - (CONV_WIDTH - 1), 0), (0, 0)))


def _lru_layer(x, T_p, B, S, Bd, Ld, state_h, state_conv, p, ln_g, ln_b, tag):
    R = p["lam"].shape[1]
    proj = _proj(x, p["w_in"], p["b_in"], f"lru_in_{tag}")
    zeros_h = jnp.zeros((B, SUBLANES, R), F32)
    hg, h_p = _lru_seq(proj, None, zeros_h, zeros_h, p, row_off=0, B=B, Lp=S,
                       Lc=_pick(S, (256, 128, 64, 8)), L_valid=None, name=f"lru_seq_p_{tag}")
    h0 = jnp.broadcast_to(state_h[:, None, :], (Bd, SUBLANES, R))
    hg, h_s = _lru_seq(proj, hg, h0, _pad_tail(state_conv), p, row_off=T_p, B=Bd, Lp=SAMPLE_PAD,
                       Lc=SAMPLE_PAD, L_valid=Ld, name=f"lru_seq_s_{tag}")
    x_new = _outproj_ln(hg, p["w_out"], x, ln_g, ln_b, f"lru_out_{tag}")
    u_s = proj[T_p:, R:].reshape(Bd, SAMPLE_PAD, R)
    conv_p = _tail_rows(proj, B, S, CONV_WIDTH - 1, R, R)
    conv_s = jnp.concatenate([state_conv, u_s[:, :Ld]], axis=1)[:, -(CONV_WIDTH - 1):]
    return x_new, h_p, h_s, conv_p, conv_s


def _gdn_layer(x, T_p, B, S, Bd, Ld, state_s, state_conv, p, ln_g, ln_b, tag):
    QK, VD = GDN_HK * GDN_DK, GDN_HV * GDN_DV
    CD = 2 * QK + VD
    proj = _proj(x, p["w_main"], p["b_main"], f"gdn_in_{tag}")
    o, s_p = _gdn_seq(proj, None, jnp.zeros((B, GDN_HV, GDN_DK, GDN_DV), F32),
                      jnp.zeros((B, SUBLANES, CD), F32), p, row_off=0, B=B, Lp=S,
                      C=math.gcd(S, GDN_CHUNK), L_valid=None, name=f"gdn_seq_p_{tag}")
    o, s_s = _gdn_seq(proj, o, state_s, _pad_tail(state_conv), p, row_off=T_p, B=Bd, Lp=SAMPLE_PAD,
                      C=SAMPLE_PAD, L_valid=Ld, name=f"gdn_seq_s_{tag}")
    x_new = _outproj_ln(o, p["w_out"], x, ln_g, ln_b, f"gdn_out_{tag}")
    q_s = proj[T_p:, :CD].reshape(Bd, SAMPLE_PAD, CD)
    conv_p = _tail_rows(proj, B, S, CONV_WIDTH - 1, 0, CD)
    conv_s = jnp.concatenate([state_conv, q_s[:, :Ld]], axis=1)[:, -(CONV_WIDTH - 1):]
    return x_new, s_p, s_s, conv_p, conv_s


def _unpermute_tail(pg, B, S, d, keep, col0, ncol):
    tiles = -(-keep // SWA_TILE)
    rows = tiles * SWA_TILE
    t = _tail_rows(pg, B, S, rows, col0, ncol)
    t = t.reshape(B, tiles, d, SWA_TILE // d, ncol).transpose(0, 1, 3, 2, 4).reshape(B, rows, ncol)
    return t[:, rows - keep:]


def _swa_layer(x, T_p, B, S, Bd, Ld, caches, p, rope_cos, rope_ss, ln_g, ln_b, tag):
    W = SWA_HEADS * SWA_HD
    T_all = x.shape[0]
    n_s = T_all - T_p
    outs, lses, outs_s, lses_s, new_p, new_s = [], [], [], [], [], []
    for gi in range(SWA_GROUPS):
        d, window = SWA_DILATIONS[gi], SWA_WINDOWS[gi]
        pg = _swa_proj(x, p["w_in"][gi], rope_cos, rope_ss, d=d, tm=SWA_TILE, row_blk_off=0,
                       n_row_blks=T_p // SWA_TILE, name=f"swa_in_p{gi}_{tag}")
        o, l = _swa_attn(pg, B=B, S=S, d=d, name=f"swa_attn_p{gi}_{tag}")
        outs.append(o)
        lses.append(l)
        keep = min(window, S)
        kv = _unpermute_tail(pg, B, S, d, keep, W, 2 * W)
        new_p.append(kv.reshape(B, keep, 2, SWA_HEADS, SWA_HD))

        ps = _swa_proj(x, p["w_in"][gi], rope_cos, rope_ss, d=1, tm=n_s, row_blk_off=T_p // n_s,
                       n_row_blks=1, name=f"swa_in_s{gi}_{tag}")
        cache = caches[gi]
        Wb = cache.shape[1]
        nres = min(d, Ld)
        cview = cache.reshape(Bd, Wb // d, d, 2 * W)[:, :, :nres].transpose(0, 2, 1, 3)
        o_s, l_s = _swa_sample_attn(ps, cview, d=d, window=window, name=f"swa_attn_s{gi}_{tag}")
        outs_s.append(o_s)
        lses_s.append(l_s)
        new_s.append(ps.reshape(Bd, SAMPLE_PAD, 3 * W)[:, :Ld, W:].reshape(Bd, Ld, 2, SWA_HEADS, SWA_HD))
    full = _swa_merge(outs, lses, rows=T_p, total_rows=T_all, name=f"swa_merge_p_{tag}")
    full = _swa_merge_sample(outs_s, lses_s, full, row_off=T_p, name=f"swa_merge_s_{tag}")
    x_new = _outproj_ln(full, p["w_out"], x, ln_g, ln_b, f"swa_out_{tag}")
    return x_new, new_p, new_s


def _rope_tables(T_p, S, n_s):
    half = SWA_HD // 2
    inv = ROPE_THETA ** (-2.0 * jnp.arange(half, dtype=F32) / SWA_HD)
    pos = jnp.concatenate([jnp.arange(T_p, dtype=I32) % S,
                           PAST_LEN + jnp.arange(n_s, dtype=I32) % SAMPLE_PAD]).astype(F32)
    ang = pos[:, None] * inv[None, :]
    cos, sin = jnp.cos(ang), jnp.sin(ang)
    return jnp.concatenate([cos, cos, cos, cos], 1), jnp.concatenate([-sin, sin, -sin, sin], 1)


def kernel(x_prompt, x_sample, state_lru_h, state_lru_conv, state_gdn_s, state_gdn_conv,
           cache_swa_w128, cache_swa_w512, cache_swa_w2048,
           lru_w_in, lru_b_in, lru_conv_w, lru_conv_b, lru_w_a, lru_b_a, lru_w_i, lru_b_i,
           lru_lambda, lru_w_out,
           gdn_w_in, gdn_conv_w, gdn_a_log, gdn_dt_bias, gdn_norm_w, gdn_w_out,
           swa_w_in, swa_w_out,
           ln1_g, ln1_b, ln2_g, ln2_b,
           router_w, router_b, moe_w_gate, moe_w_up, moe_w_down):
    B, S, D = x_prompt.shape
    Bd, Ld, _ = x_sample.shape
    T_p = B * S
    n_s = Bd * SAMPLE_PAD
    assert S % SWA_TILE == 0 and Ld <= SAMPLE_PAD and T_p % n_s == 0
    xs_pad = jnp.pad(x_sample, ((0, 0), (0, SAMPLE_PAD - Ld), (0, 0)))
    x = jnp.concatenate([x_prompt.reshape(T_p, D), xs_pad.reshape(n_s, D)], axis=0)

    row = lambda v: v.reshape(1, -1)
    pad8 = lambda w: jnp.pad(w, ((0, SUBLANES - w.shape[0]), (0, 0)))
    rw_pad = jnp.pad(router_w, ((0, 0), (0, LANES - N_EXPERTS)))
    rb_pad = jnp.concatenate([router_b, jnp.full((LANES - N_EXPERTS,), -1e30, F32)]).reshape(1, LANES)
    rope_cos, rope_ss = _rope_tables(T_p, S, n_s)
    caches = (cache_swa_w128, cache_swa_w512, cache_swa_w2048)

    res = {k: [] for k in ("lru_h_p", "lru_h_s", "lru_c_p", "lru_c_s", "gdn_s_p", "gdn_s_s", "gdn_c_p",
                           "gdn_c_s")}
    swa_p = [[] for _ in range(SWA_GROUPS)]
    swa_s = [[] for _ in range(SWA_GROUPS)]
    for i in range(DEPTH):
        kind, j = LAYER_KIND[i], KIND_INDEX[i]
        g1, b1 = row(ln1_g[i]), row(ln1_b[i])
        if kind == 0:
            p = dict(w_in=lru_w_in[j].astype(BF16), b_in=row(lru_b_in[j]), cw=pad8(lru_conv_w[j]),
                     cb=row(lru_conv_b[j]),
                     wai=jnp.concatenate([lru_w_a[j], lru_w_i[j]], axis=-1).astype(BF16),
                     ba=row(lru_b_a[j]), bi=row(lru_b_i[j]), lam=row(lru_lambda[j]),
                     w_out=lru_w_out[j].astype(BF16))
            x, h_p, h_s, c_p, c_s = _lru_layer(x, T_p, B, S, Bd, Ld, state_lru_h[j], state_lru_conv[j], p,
                                               g1, b1, f"l{i}")
            res["lru_h_p"].append(h_p)
            res["lru_h_s"].append(h_s)
            res["lru_c_p"].append(c_p)
            res["lru_c_s"].append(c_s)
        elif kind == 1:
            QK, VD = GDN_HK * GDN_DK, GDN_HV * GDN_DV
            CD = 2 * QK + VD
            w_in = gdn_w_in[j]
            p = dict(w_main=jnp.pad(w_in, ((0, 0), (0, LANES - 2 * GDN_HV))).astype(BF16),
                     b_main=jnp.zeros((1, CD + VD + LANES), F32),
                     cw=pad8(gdn_conv_w[j]),
                     alog=jnp.pad(gdn_a_log[j], (GDN_HV, LANES - 2 * GDN_HV)).reshape(1, LANES),
                     dtb=jnp.pad(gdn_dt_bias[j], (GDN_HV, LANES - 2 * GDN_HV)).reshape(1, LANES),
                     nw=row(gdn_norm_w[j]), w_out=gdn_w_out[j].astype(BF16))
            x, s_p, s_s, c_p, c_s = _gdn_layer(x, T_p, B, S, Bd, Ld, state_gdn_s[j], state_gdn_conv[j], p,
                                               g1, b1, f"l{i}")
            res["gdn_s_p"].append(s_p)
            res["gdn_s_s"].append(s_s)
            res["gdn_c_p"].append(c_p)
            res["gdn_c_s"].append(c_s)
        else:
            W = SWA_HEADS * SWA_HD
            w_in = swa_w_in[j].astype(BF16)
            p = dict(w_in=[w_in[:, gi * 3 * W:(gi + 1) * 3 * W] for gi in range(SWA_GROUPS)],
                     w_out=swa_w_out[j].astype(BF16))
            x, new_p, new_s = _swa_layer(x, T_p, B, S, Bd, Ld, tuple(c[j] for c in caches), p,
                                         rope_cos, rope_ss, g1, b1, f"l{i}")
            for gi in range(SWA_GROUPS):
                swa_p[gi].append(new_p[gi])
                swa_s[gi].append(new_s[gi])
        x = _moe_ln(x, rw_pad, rb_pad, moe_w_gate, moe_w_up, moe_w_down, i,
                    row(ln2_g[i]), row(ln2_b[i]), f"l{i}")

    y_prompt = x[:T_p].reshape(B, S, D)
    y_sample = x[T_p:].reshape(Bd, SAMPLE_PAD, D)[:, :Ld]
    st = lambda k: jnp.stack(res[k])
    return (y_prompt, y_sample, st("lru_h_p"), st("lru_h_s"), st("lru_c_p"), st("lru_c_s"),
            st("gdn_s_p"), st("gdn_s_s"), st("gdn_c_p"), st("gdn_c_s"),
            jnp.stack(swa_p[0]), jnp.stack(swa_s[0]), jnp.stack(swa_p[1]), jnp.stack(swa_s[1]),
            jnp.stack(swa_p[2]), jnp.stack(swa_s[2]))
```

```python
import functools
import math

import jax
import jax.numpy as jnp
from jax import lax
from jax.experimental import pallas as pl
from jax.experimental.pallas import tpu as pltpu

F32 = jnp.float32
BF16 = jnp.bfloat16
I32 = jnp.int32

DEPTH = 4
LAYER_KIND = (0, 1, 2, 0)
KIND_INDEX = (0, 0, 0, 1)
DN_ALPHA = (2 * DEPTH) ** 0.25
LN_EPS = 1e-5
NORM_EPS = 1e-6
CONV_WIDTH = 4
LRU_BLOCKS = 4
LRU_C = 8.0
GDN_HK = 8
GDN_HV = 16
GDN_DK = 128
GDN_DV = 128
GDN_CHUNK = 64
SWA_WINDOWS = (128, 512, 2048)
SWA_DILATIONS = (1, 4, 16)
SWA_GROUPS = 3
SWA_HEADS = 16
SWA_HD = 64
SWA_BLOCK = 128
ROPE_THETA = 10000.0
PAST_LEN = 8192
N_EXPERTS = 32
N_EXPERT_GROUPS = 4
EXPERTS_PER_GROUP = 8

LANES = 128
SUBLANES = 8
SAMPLE_PAD = 8
SWA_TILE = 2048
MOE_ROWS = 256
TOK_BLK = 128
VMEM_LIMIT = 56 * 1024 * 1024


def _cparams(sem):
    return pltpu.CompilerParams(dimension_semantics=sem, vmem_limit_bytes=VMEM_LIMIT)


def _pick(n, cands):
    for c in cands:
        if n % c == 0:
            return c
    raise ValueError(f"no tile for {n}")


def _bdot(a, b):
    return jnp.dot(a.astype(BF16), b.astype(BF16), preferred_element_type=F32)


def _nt(a, b):
    return lax.dot_general(a.astype(BF16), b.astype(BF16), (((1,), (1,)), ((), ())),
                           preferred_element_type=F32)


def _tn(a, b):
    return lax.dot_general(a.astype(BF16), b.astype(BF16), (((0,), (0,)), ((), ())),
                           preferred_element_type=F32)


def _split(a):
    hi = a.astype(BF16)
    return hi, (a - hi.astype(F32)).astype(BF16)


def _dot3(a, b):
    (ah, al), (bh, bl) = a, b
    return (jnp.dot(ah, bh, preferred_element_type=F32)
            + (jnp.dot(ah, bl, preferred_element_type=F32) + jnp.dot(al, bh, preferred_element_type=F32)))


def _sigmoid(x):
    return 1.0 / (1.0 + jnp.exp(-x))


def _softplus(x):
    return jnp.maximum(x, 0.0) + jnp.log(1.0 + jnp.exp(-jnp.abs(x)))


def _layer_norm(x, g, b):
    mu = jnp.mean(x, -1, keepdims=True)
    xc = x - mu
    var = jnp.mean(xc * xc, -1, keepdims=True)
    return xc * lax.rsqrt(var + LN_EPS) * g + b


def _row_scan_sum(x, n):
    rows = lax.broadcasted_iota(I32, x.shape, 0)
    s = 1
    while s < n:
        x = x + jnp.where(rows >= s, pltpu.roll(x, s, 0), 0.0)
        s *= 2
    return x


def _proj_kernel(x_ref, w_ref, b_ref, o_ref):
    o_ref[...] = jnp.dot(x_ref[...].astype(BF16), w_ref[...], preferred_element_type=F32) + b_ref[...]


def _proj(x, w_bf, b, name):
    M, K = x.shape
    N = w_bf.shape[1]
    tm = _pick(M, (1280, 1024, 512, 256, 128, 64, 8))
    tn = _pick(N, (1024, 896, 768, 512, 384, 256, 128))
    return pl.pallas_call(
        _proj_kernel, grid=(M // tm, N // tn),
        in_specs=[pl.BlockSpec((tm, K), lambda i, j: (i, 0)),
                  pl.BlockSpec((K, tn), lambda i, j: (0, j)),
                  pl.BlockSpec((1, tn), lambda i, j: (0, j))],
        out_specs=pl.BlockSpec((tm, tn), lambda i, j: (i, j)),
        out_shape=jax.ShapeDtypeStruct((M, N), F32),
        compiler_params=_cparams(("parallel", "parallel")), name=name)(x, w_bf, b)


def _outproj_ln_kernel(h_ref, w_ref, r_ref, g_ref, b_ref, o_ref):
    y = jnp.dot(h_ref[...].astype(BF16), w_ref[...], preferred_element_type=F32)
    o_ref[...] = _layer_norm(DN_ALPHA * r_ref[...] + y, g_ref[...], b_ref[...])


def _outproj_ln(h, w_bf, res, g, b, name):
    M, K = h.shape
    N = w_bf.shape[1]
    tm = _pick(M, (1280, 1024, 512, 256, 128, 64, 8))
    return pl.pallas_call(
        _outproj_ln_kernel, grid=(M // tm,),
        in_specs=[pl.BlockSpec((tm, K), lambda i: (i, 0)),
                  pl.BlockSpec((K, N), lambda i: (0, 0)),
                  pl.BlockSpec((tm, N), lambda i: (i, 0)),
                  pl.BlockSpec((1, N), lambda i: (0, 0)),
                  pl.BlockSpec((1, N), lambda i: (0, 0))],
        out_specs=pl.BlockSpec((tm, N), lambda i: (i, 0)),
        out_shape=jax.ShapeDtypeStruct((M, N), F32),
        compiler_params=_cparams(("parallel",)), name=name)(h, w_bf, res, g, b)


def _lru_kernel(gate_ref, u_ref, h0_ref, cb0_ref, cw_ref, cbias_ref, wai_ref, ba_ref, bi_ref,
                lam_ref, o_ref, hl_ref, ext, hcar, *, Lc, L_valid, R):
    c = pl.program_id(1)

    @pl.when(c == 0)
    def _():
        ext[0:SUBLANES, :] = cb0_ref[0]
        hcar[...] = h0_ref[0][0:1, :]

    @pl.when(c > 0)
    def _():
        ext[0:SUBLANES, :] = ext[Lc:Lc + SUBLANES, :]

    ext[SUBLANES:SUBLANES + Lc, :] = u_ref[...]
    u = cbias_ref[...] + cw_ref[3:4, :] * ext[8:8 + Lc, :]
    for k in range(CONV_WIDTH - 1):
        u = u + cw_ref[k:k + 1, :] * ext[5 + k:5 + k + Lc, :]

    bw = R // LRU_BLOCKS
    za, zi = [], []
    for n in range(LRU_BLOCKS):
        z = jnp.dot(u[:, n * bw:(n + 1) * bw].astype(BF16), wai_ref[n], preferred_element_type=F32)
        za.append(z[:, :bw])
        zi.append(z[:, bw:])
    r = _sigmoid(jnp.concatenate(za, axis=1) + ba_ref[...])
    ig = _sigmoid(jnp.concatenate(zi, axis=1) + bi_ref[...])
    log_a = (-LRU_C) * r * _softplus(-lam_ref[...])
    a = jnp.exp(log_a)
    th = jnp.tanh(log_a)
    xin = jnp.sqrt(-2.0 * th / (1.0 - th)) * ig * u

    rows = lax.broadcasted_iota(I32, (Lc, R), 0)
    if L_valid is not None:
        valid = (rows + c * Lc) < L_valid
        a = jnp.where(valid, a, 1.0)
        xin = jnp.where(valid, xin, 0.0)

    s = 1
    while s < Lc:
        a_sh = jnp.where(rows >= s, pltpu.roll(a, s, 0), 1.0)
        b_sh = jnp.where(rows >= s, pltpu.roll(xin, s, 0), 0.0)
        xin = a * b_sh + xin
        a = a * a_sh
        s *= 2
    h = a * hcar[...] + xin
    hcar[...] = h[Lc - 1:Lc, :]
    hl_ref[0] = jnp.broadcast_to(h[Lc - 1:Lc, :], (SUBLANES, R))
    o_ref[...] = h * jax.nn.gelu(gate_ref[...], approximate=True)


def _lru_seq(proj, prev_out, h0, cb0, p, *, row_off, B, Lp, Lc, L_valid, name):
    T_all, R2 = proj.shape
    R = R2 // 2
    nC = Lp // Lc
    off = row_off // Lc
    kern = functools.partial(_lru_kernel, Lc=Lc, L_valid=L_valid, R=R)
    row_map = lambda b, c: (off + b * nC + c, 0)
    const2 = lambda b, c: (0, 0)
    in_specs = [pl.BlockSpec((Lc, R), row_map),
                pl.BlockSpec((Lc, R), lambda b, c: (off + b * nC + c, 1)),
                pl.BlockSpec((1, SUBLANES, R), lambda b, c: (b, 0, 0)),
                pl.BlockSpec((1, SUBLANES, R), lambda b, c: (b, 0, 0)),
                pl.BlockSpec((SUBLANES, R), const2),
                pl.BlockSpec((1, R), const2),
                pl.BlockSpec((LRU_BLOCKS, R // LRU_BLOCKS, 2 * R // LRU_BLOCKS), lambda b, c: (0, 0, 0)),
                pl.BlockSpec((1, R), const2), pl.BlockSpec((1, R), const2), pl.BlockSpec((1, R), const2)]
    args = [proj, proj, h0, cb0, p["cw"], p["cb"], p["wai"], p["ba"], p["bi"], p["lam"]]
    aliases = {}
    if prev_out is not None:
        in_specs.append(pl.BlockSpec(memory_space=pl.ANY))
        args.append(prev_out)
        aliases = {len(args) - 1: 0}
        kern = _drop_extra_input(kern, n_in=10, n_extra=1)
    out, hl = pl.pallas_call(
        kern, grid=(B, nC), in_specs=in_specs,
        out_specs=[pl.BlockSpec((Lc, R), row_map), pl.BlockSpec((1, SUBLANES, R), lambda b, c: (b, 0, 0))],
        out_shape=[jax.ShapeDtypeStruct((T_all, R), F32), jax.ShapeDtypeStruct((B, SUBLANES, R), F32)],
        scratch_shapes=[pltpu.VMEM((Lc + SUBLANES, R), F32), pltpu.VMEM((1, R), F32)],
        input_output_aliases=aliases,
        compiler_params=_cparams(("parallel", "arbitrary")), name=name)(*args)
    return out, hl[:, 0, :]


def _drop_extra_input(kern, n_in, n_extra):
    def wrapped(*refs):
        return kern(*refs[:n_in], *refs[n_in + n_extra:])
    return wrapped


def _gdn_kernel(qkv_ref, z_ref, ba_ref, s0_ref, cb0_ref, cw_ref, alog_ref, dtb_ref, nw_ref,
                o_ref, sfin_ref, ext, S, *, C, L_valid, nC):
    c = pl.program_id(1)
    QK = GDN_HK * GDN_DK

    @pl.when(c == 0)
    def _():
        ext[0:SUBLANES, :] = cb0_ref[0]
        S[...] = s0_ref[0]

    @pl.when(c > 0)
    def _():
        ext[0:SUBLANES, :] = ext[C:C + SUBLANES, :]

    ext[SUBLANES:SUBLANES + C, :] = qkv_ref[...]
    xc = cw_ref[3:4, :] * ext[8:8 + C, :]
    for k in range(CONV_WIDTH - 1):
        xc = xc + cw_ref[k:k + 1, :] * ext[5 + k:5 + k + C, :]
    act = xc * _sigmoid(xc)

    ba = ba_ref[...]
    beta = _sigmoid(ba)
    g = pltpu.roll(-jnp.exp(alog_ref[...]) * _softplus(ba + dtb_ref[...]), LANES - GDN_HV, 1)
    if L_valid is not None:
        rows = lax.broadcasted_iota(I32, (C, LANES), 0)
        valid = (rows + c * C) < L_valid
        g = jnp.where(valid, g, 0.0)
        beta = jnp.where(valid, beta, 0.0)
    gc = _row_scan_sum(g, C)
    gct = jnp.transpose(gc)
    e_gc = jnp.exp(gc)
    g_last = gc[C - 1:C, :]
    e_rest = jnp.exp(g_last - gc)
    e_last = jnp.exp(g_last)

    ri = lax.broadcasted_iota(I32, (C, C), 0)
    ci = lax.broadcasted_iota(I32, (C, C), 1)
    tril = ri >= ci
    strict = ri > ci

    qn, kn = [], []
    for hk in range(GDN_HK):
        qh = act[:, hk * GDN_DK:(hk + 1) * GDN_DK]
        kh = act[:, QK + hk * GDN_DK:QK + (hk + 1) * GDN_DK]
        qn.append(qh * lax.rsqrt(jnp.sum(qh * qh, -1, keepdims=True) + NORM_EPS) * (GDN_DK ** -0.5))
        kn.append(kh * lax.rsqrt(jnp.sum(kh * kh, -1, keepdims=True) + NORM_EPS))

    rep = GDN_HV // GDN_HK
    heads = range(GDN_HV)
    qkk = [_nt(jnp.concatenate([qn[hk], kn[hk]], axis=0), kn[hk]) for hk in range(GDN_HK)]
    eye = jnp.where(ri == ci, 1.0, 0.0)
    decay, xs, ps, rhs = [], [], [], []
    for h in heads:
        kh = kn[h // rep]
        vh = act[:, 2 * QK + h * GDN_DV:2 * QK + (h + 1) * GDN_DV]
        bi = beta[:, h:h + 1]
        dec = jnp.exp(jnp.where(tril, gc[:, h:h + 1] - gct[h:h + 1, :], -jnp.inf))
        a_mat = jnp.where(strict, bi * qkk[h // rep][C:, :] * dec, 0.0)
        decay.append(dec)
        xs.append(eye - a_mat)
        ps.append(a_mat)
        rhs.append(jnp.concatenate([vh * bi, kh * (bi * e_gc[:, h:h + 1])], axis=1))
    p_sp = [_split(p) for p in ps]
    k = 2
    while k < C:
        ps = [_dot3(sp, sp) for sp in p_sp]
        p_sp = [_split(p) for p in ps]
        xs = [x + _dot3(_split(x), sp) for x, sp in zip(xs, p_sp)]
        k *= 2
    uw = [_bdot(x, r) for x, r in zip(xs, rhs)]
    ws = [_bdot(jnp.concatenate([uw[h][:, GDN_DV:], qn[h // rep] * e_gc[:, h:h + 1]], axis=0), S[h])
          for h in heads]
    v_new = [uw[h][:, :GDN_DV] - ws[h][:C, :] for h in heads]
    o_intra = [_bdot(qkk[h // rep][:C, :] * decay[h], v_new[h]) for h in heads]
    s_upd = [_tn(kn[h // rep] * e_rest[:, h:h + 1], v_new[h]) for h in heads]
    for h in heads:
        S[h] = S[h] * e_last[:, h:h + 1] + s_upd[h]
        o = ws[h][C:, :] + o_intra[h]
        o = o * lax.rsqrt(jnp.mean(o * o, -1, keepdims=True) + NORM_EPS) * nw_ref[...]
        zh = z_ref[:, h * GDN_DV:(h + 1) * GDN_DV]
        o_ref[:, h * GDN_DV:(h + 1) * GDN_DV] = o * (zh * _sigmoid(zh))

    @pl.when(c == nC - 1)
    def _():
        sfin_ref[0] = S[...]


def _gdn_seq(proj, prev_out, s0, cb0, p, *, row_off, B, Lp, C, L_valid, name):
    T_all = proj.shape[0]
    QK, VD = GDN_HK * GDN_DK, GDN_HV * GDN_DV
    CD = 2 * QK + VD
    nC = Lp // C
    off = row_off // C
    kern = functools.partial(_gdn_kernel, C=C, L_valid=L_valid, nC=nC)
    row_map = lambda b, c: (off + b * nC + c, 0)
    const2 = lambda b, c: (0, 0)
    zblk = CD // VD
    in_specs = [pl.BlockSpec((C, CD), row_map),
                pl.BlockSpec((C, VD), lambda b, c: (off + b * nC + c, zblk)),
                pl.BlockSpec((C, LANES), lambda b, c: (off + b * nC + c, (CD + VD) // LANES)),
                pl.BlockSpec((1, GDN_HV, GDN_DK, GDN_DV), lambda b, c: (b, 0, 0, 0)),
                pl.BlockSpec((1, SUBLANES, CD), lambda b, c: (b, 0, 0)),
                pl.BlockSpec((SUBLANES, CD), const2),
                pl.BlockSpec((1, LANES), const2), pl.BlockSpec((1, LANES), const2),
                pl.BlockSpec((1, GDN_DV), const2)]
    args = [proj, proj, proj, s0, cb0, p["cw"], p["alog"], p["dtb"], p["nw"]]
    aliases = {}
    if prev_out is not None:
        in_specs.append(pl.BlockSpec(memory_space=pl.ANY))
        args.append(prev_out)
        aliases = {len(args) - 1: 0}
        kern = _drop_extra_input(kern, n_in=9, n_extra=1)
    out, sfin = pl.pallas_call(
        kern, grid=(B, nC), in_specs=in_specs,
        out_specs=[pl.BlockSpec((C, VD), row_map),
                   pl.BlockSpec((1, GDN_HV, GDN_DK, GDN_DV), lambda b, c: (b, 0, 0, 0))],
        out_shape=[jax.ShapeDtypeStruct((T_all, VD), F32),
                   jax.ShapeDtypeStruct((B, GDN_HV, GDN_DK, GDN_DV), F32)],
        scratch_shapes=[pltpu.VMEM((C + SUBLANES, CD), F32), pltpu.VMEM((GDN_HV, GDN_DK, GDN_DV), F32)],
        input_output_aliases=aliases,
        compiler_params=_cparams(("parallel", "arbitrary")), name=name)(*args)
    return out, sfin


def _rope(x, cos, ss):
    W = x.shape[1]
    reps = W // LANES
    cfull = jnp.concatenate([cos] * reps, axis=1)
    sfull = jnp.concatenate([ss] * reps, axis=1)
    lane = lax.broadcasted_iota(I32, x.shape, 1)
    first = (lane % SWA_HD) < (SWA_HD // 2)
    partner = jnp.where(first, pltpu.roll(x, W - SWA_HD // 2, 1), pltpu.roll(x, SWA_HD // 2, 1))
    return x * cfull + partner * sfull


def _swa_proj_kernel(x_ref, w_ref, cos_ref, ss_ref, o_ref, acc_ref, *, d, tm, n_rot):
    j = pl.program_id(1)
    acc = jnp.dot(x_ref[...].astype(BF16), w_ref[...], preferred_element_type=F32)

    def emit(val):
        if d == 1:
            o_ref[...] = val
            return
        n = tm // d
        for cb in range(val.shape[1] // LANES):
            acc_ref[cb] = val[:, cb * LANES:(cb + 1) * LANES]
        for cb in range(val.shape[1] // LANES):
            for r in range(d):
                o_ref[r * n:(r + 1) * n, cb * LANES:(cb + 1) * LANES] = acc_ref[cb, pl.ds(r, n, stride=d), :]

    @pl.when(j < n_rot)
    def _():
        emit(_rope(acc, cos_ref[...], ss_ref[...]))

    @pl.when(j >= n_rot)
    def _():
        emit(acc)


def _swa_proj(x, w_bf, cos, ss, *, d, tm, row_blk_off, n_row_blks, name):
    K = x.shape[1]
    N = w_bf.shape[1]
    tn = 512
    n_rot = (2 * N // 3) // tn
    kern = functools.partial(_swa_proj_kernel, d=d, tm=tm, n_rot=n_rot)
    return pl.pallas_call(
        kern, grid=(n_row_blks, N // tn),
        in_specs=[pl.BlockSpec((tm, K), lambda i, j: (i + row_blk_off, 0)),
                  pl.BlockSpec((K, tn), lambda i, j: (0, j)),
                  pl.BlockSpec((tm, LANES), lambda i, j: (i + row_blk_off, 0)),
                  pl.BlockSpec((tm, LANES), lambda i, j: (i + row_blk_off, 0))],
        out_specs=pl.BlockSpec((tm, tn), lambda i, j: (i, j)),
        out_shape=jax.ShapeDtypeStruct((n_row_blks * tm, N), F32),
        scratch_shapes=[pltpu.VMEM((tn // LANES, tm, LANES), F32)],
        compiler_params=_cparams(("parallel", "parallel")), name=name)(x, w_bf, cos, ss)


def _swa_attn_kernel(q_ref, kc_ref, kp_ref, vc_ref, vp_ref, o_ref, l_ref, *, nbt):
    t = pl.program_id(1)
    j = pl.program_id(2)
    has_prev = jnp.logical_or(t > 0, (j % nbt) > 0)
    QB = SWA_BLOCK
    qi = lax.broadcasted_iota(I32, (QB, QB), 0)
    ki = lax.broadcasted_iota(I32, (QB, QB), 1)
    mask_c = ki <= qi
    mask_p = jnp.logical_and(ki >= qi, has_prev)
    sls = [slice(h * SWA_HD, (h + 1) * SWA_HD) for h in range(SWA_HEADS)]
    qs = [(q_ref[:, sl] * (SWA_HD ** -0.5)).astype(BF16) for sl in sls]
    scs = [_nt(q, kc_ref[:, sl]) for q, sl in zip(qs, sls)]
    sps = [_nt(q, kp_ref[:, sl]) for q, sl in zip(qs, sls)]
    pcs, pps, ms, ls = [], [], [], []
    for sc, sp in zip(scs, sps):
        sc = jnp.where(mask_c, sc, -jnp.inf)
        sp = jnp.where(mask_p, sp, -jnp.inf)
        m = jnp.maximum(jnp.max(sc, -1, keepdims=True), jnp.max(sp, -1, keepdims=True))
        pc = jnp.exp(sc - m)
        pp = jnp.exp(sp - m)
        l = jnp.sum(pc, -1, keepdims=True) + jnp.sum(pp, -1, keepdims=True)
        inv_l = 1.0 / l
        ls.append(l)
        ms.append(m)
        pcs.append((pc * inv_l).astype(BF16))
        pps.append((pp * inv_l).astype(BF16))
    ocs = [_bdot(pc, vc_ref[:, sl]) for pc, sl in zip(pcs, sls)]
    ops = [_bdot(pp, vp_ref[:, sl]) for pp, sl in zip(pps, sls)]
    for h, sl in enumerate(sls):
        o_ref[:, sl] = ocs[h] + ops[h]
        l_ref[:, sl] = jnp.broadcast_to(ms[h] + jnp.log(ls[h]), (QB, SWA_HD))


def _swa_attn(pg, *, B, S, d, name):
    W = SWA_HEADS * SWA_HD
    QB = SWA_BLOCK
    bpt = SWA_TILE // QB
    nbt = bpt // d
    nT = S // SWA_TILE
    nblk = S // QB

    def cur(col):
        return lambda b, t, j: (b * nblk + t * bpt + j, col)

    def prev(col):
        def f(b, t, j):
            first = (j % nbt) == 0
            pj = jnp.where(first, j + nbt - 1, j - 1)
            pt = jnp.where(first, jnp.maximum(t - 1, 0), t)
            return (b * nblk + pt * bpt + pj, col)
        return f

    blk = (QB, W)
    o, l = pl.pallas_call(
        functools.partial(_swa_attn_kernel, nbt=nbt), grid=(B, nT, bpt),
        in_specs=[pl.BlockSpec(blk, cur(0)), pl.BlockSpec(blk, cur(1)), pl.BlockSpec(blk, prev(1)),
                  pl.BlockSpec(blk, cur(2)), pl.BlockSpec(blk, prev(2))],
        out_specs=[pl.BlockSpec(blk, cur(0)), pl.BlockSpec(blk, cur(0))],
        out_shape=[jax.ShapeDtypeStruct((B * S, W), F32), jax.ShapeDtypeStruct((B * S, W), F32)],
        compiler_params=_cparams(("parallel", "parallel", "arbitrary")), name=name)(pg, pg, pg, pg, pg)
    return o, l


def _merge3(o0, l0, o1, l1, o2, l2):
    m = jnp.maximum(jnp.maximum(l0, l1), l2)
    w0 = jnp.exp(l0 - m)
    w1 = jnp.exp(l1 - m)
    w2 = jnp.exp(l2 - m)
    return (w0 * o0 + w1 * o1 + w2 * o2) / (w0 + w1 + w2)


def _swa_merge_kernel(o0, l0, o1, l1, o2, l2, out_ref, s1o, s1l, s2o, s2l, *, tm):
    for (src_o, src_l, dst_o, dst_l, d) in ((o1, l1, s1o, s1l, SWA_DILATIONS[1]),
                                            (o2, l2, s2o, s2l, SWA_DILATIONS[2])):
        n = tm // d
        for cb in range(dst_o.shape[0]):
            cs = slice(cb * LANES, (cb + 1) * LANES)
            for r in range(d):
                dst_o[cb, pl.ds(r, n, stride=d), :] = src_o[r * n:(r + 1) * n, cs]
                dst_l[cb, pl.ds(r, n, stride=d), :] = src_l[r * n:(r + 1) * n, cs]
    for cb in range(s1o.shape[0]):
        cs = slice(cb * LANES, (cb + 1) * LANES)
        out_ref[:, cs] = _merge3(o0[:, cs], l0[:, cs], s1o[cb], s1l[cb], s2o[cb], s2l[cb])


def _swa_merge(outs, lses, *, rows, total_rows, name):
    W = outs[0].shape[1]
    tm, tn = SWA_TILE, 256
    spec = pl.BlockSpec((tm, tn), lambda i, j: (i, j))
    args = [outs[0], lses[0], outs[1], lses[1], outs[2], lses[2]]
    return pl.pallas_call(
        functools.partial(_swa_merge_kernel, tm=tm), grid=(rows // tm, W // tn),
        in_specs=[spec] * 6, out_specs=spec,
        out_shape=jax.ShapeDtypeStruct((total_rows, W), F32),
        scratch_shapes=[pltpu.VMEM((tn // LANES, tm, LANES), F32)] * 4,
        compiler_params=_cparams(("parallel", "parallel")), name=name)(*args)


def _swa_sample_kernel(x_ref, c_ref, o_ref, l_ref, *, d, window, Wb):
    L = SAMPLE_PAD
    W = SWA_HEADS * SWA_HD
    ccol = lax.broadcasted_iota(I32, (L, Wb), 1)
    lrow = lax.broadcasted_iota(I32, (L, Wb), 0)
    dist = Wb + lrow - ccol
    mask_c = jnp.logical_and(dist % d == 0, dist <= window)
    ln = lax.broadcasted_iota(I32, (L, L), 0)
    lk = lax.broadcasted_iota(I32, (L, L), 1)
    dn = ln - lk
    mask_n = jnp.logical_and(jnp.logical_and(dn >= 0, dn % d == 0), dn <= window)
    heads = range(SWA_HEADS)
    sls = [slice(h * SWA_HD, (h + 1) * SWA_HD) for h in heads]
    qs = [(x_ref[:, sl] * (SWA_HD ** -0.5)).astype(BF16) for sl in sls]
    scs = [_bdot(qs[h], c_ref[0, h]) for h in heads]
    sns = [_nt(qs[h], x_ref[:, W + h * SWA_HD:W + (h + 1) * SWA_HD]) for h in heads]
    pcs, pns, ms, ls = [], [], [], []
    for sc, sn in zip(scs, sns):
        sc = jnp.where(mask_c, sc, -jnp.inf)
        sn = jnp.where(mask_n, sn, -jnp.inf)
        m = jnp.maximum(jnp.max(sc, -1, keepdims=True), jnp.max(sn, -1, keepdims=True))
        pc = jnp.exp(sc - m)
        pn = jnp.exp(sn - m)
        l = jnp.sum(pc, -1, keepdims=True) + jnp.sum(pn, -1, keepdims=True)
        inv_l = 1.0 / l
        ls.append(l)
        ms.append(m)
        pcs.append((pc * inv_l).astype(BF16))
        pns.append((pn * inv_l).astype(BF16))
    ocs = [_nt(pcs[h], c_ref[1, h]) for h in heads]
    ons = [_bdot(pns[h], x_ref[:, 2 * W + h * SWA_HD:2 * W + (h + 1) * SWA_HD]) for h in heads]
    for h, sl in enumerate(sls):
        o_ref[:, sl] = ocs[h] + ons[h]
        l_ref[:, sl] = jnp.broadcast_to(ms[h] + jnp.log(ls[h]), (L, SWA_HD))


def _swa_sample_attn(ps, cache, *, d, window, name):
    Bd, Wb = cache.shape[0], cache.shape[1]
    H, hd = SWA_HEADS, SWA_HD
    W = H * hd
    L = SAMPLE_PAD
    assert Wb % d == 0
    ct = jnp.transpose(cache, (0, 2, 3, 4, 1))
    kern = functools.partial(_swa_sample_kernel, d=d, window=window, Wb=Wb)
    oblk = pl.BlockSpec((L, W), lambda b: (b, 0))
    o, l = pl.pallas_call(
        kern, grid=(Bd,),
        in_specs=[pl.BlockSpec((L, 3 * W), lambda b: (b, 0)),
                  pl.BlockSpec((None, 2, H, hd, Wb), lambda b: (b, 0, 0, 0, 0))],
        out_specs=[oblk, oblk],
        out_shape=[jax.ShapeDtypeStruct((Bd * L, W), F32)] * 2,
        compiler_params=_cparams(("parallel",)), name=name)(ps, ct)
    return o, l


def _merge_plain_kernel(o0, l0, o1, l1, o2, l2, out_ref):
    out_ref[...] = _merge3(o0[...], l0[...], o1[...], l1[...], o2[...], l2[...])


def _swa_merge_sample(outs, lses, prev_out, *, row_off, name):
    rows, W = outs[0].shape
    T_all = prev_out.shape[0]
    off = row_off // rows
    spec = pl.BlockSpec((rows, W), lambda i: (0, 0))
    args = [outs[0], lses[0], outs[1], lses[1], outs[2], lses[2], prev_out]
    kern = _drop_extra_input(_merge_plain_kernel, n_in=6, n_extra=1)
    return pl.pallas_call(
        kern, grid=(1,),
        in_specs=[spec] * 6 + [pl.BlockSpec(memory_space=pl.ANY)],
        out_specs=pl.BlockSpec((rows, W), lambda i: (off, 0)),
        out_shape=jax.ShapeDtypeStruct((T_all, W), F32),
        input_output_aliases={6: 0},
        compiler_params=_cparams(("arbitrary",)), name=name)(*args)


def _router_kernel(x_ref, w_ref, b_ref, info_ref, cnt_ref, car, *, tm):
    i = pl.program_id(0)

    @pl.when(i == 0)
    def _():
        car[...] = jnp.zeros_like(car)

    logits = jnp.dot(x_ref[...], w_ref[...], precision=lax.Precision.HIGHEST,
                     preferred_element_type=F32) + b_ref[...]
    m = jnp.max(logits, -1, keepdims=True)
    e = jnp.exp(logits - m)
    probs = e / jnp.sum(e, -1, keepdims=True)
    lane = lax.broadcasted_iota(I32, (tm, LANES), 1).astype(F32)
    BIG = 4.0 * LANES

    best = None
    for gi in range(N_EXPERT_GROUPS):
        lo = float(gi * EXPERTS_PER_GROUP)
        masked = jnp.where(jnp.logical_and(lane >= lo, lane < lo + EXPERTS_PER_GROUP), probs, -1.0)
        m1 = jnp.max(masked, -1, keepdims=True)
        i1 = jnp.min(jnp.where(masked == m1, lane, BIG), -1, keepdims=True)
        masked2 = jnp.where(lane == i1, -1.0, masked)
        m2 = jnp.max(masked2, -1, keepdims=True)
        i2 = jnp.min(jnp.where(masked2 == m2, lane, BIG), -1, keepdims=True)
        score = m1 + m2
        if best is None:
            best = (score, m1, i1, m2, i2)
        else:
            take = score > best[0]
            best = tuple(jnp.where(take, n, o) for n, o in zip((score, m1, i1, m2, i2), best))
    _, m1, i1, m2, i2 = best
    wsum = m1 + m2
    w1 = m1 / wsum
    w2 = m2 / wsum

    onehot = jnp.where(jnp.logical_or(lane == i1, lane == i2), 1.0, 0.0)
    ri = lax.broadcasted_iota(I32, (tm, tm), 0)
    ci = lax.broadcasted_iota(I32, (tm, tm), 1)
    lower = jnp.where(ri > ci, 1.0, 0.0)
    before = _bdot(lower, onehot) + car[...]
    r1 = jnp.sum(jnp.where(lane == i1, before, 0.0), -1, keepdims=True)
    r2 = jnp.sum(jnp.where(lane == i2, before, 0.0), -1, keepdims=True)
    car[...] = car[...] + jnp.sum(onehot, 0, keepdims=True)
    cnt_ref[...] = jnp.broadcast_to(car[...], cnt_ref.shape)

    info = jnp.zeros((tm, LANES), F32)
    for k, val in enumerate((i1, i2, w1, w2, r1, r2)):
        info = jnp.where(lane == float(k), val, info)
    info_ref[...] = info


def _router(x, rw_pad, rb_pad, name):
    T, D = x.shape
    tm = _pick(T, (256, 128, 64, 8))
    info, cnt = pl.pallas_call(
        functools.partial(_router_kernel, tm=tm), grid=(T // tm,),
        in_specs=[pl.BlockSpec((tm, D), lambda i: (i, 0)),
                  pl.BlockSpec((D, LANES), lambda i: (0, 0)),
                  pl.BlockSpec((1, LANES), lambda i: (0, 0))],
        out_specs=[pl.BlockSpec((tm, LANES), lambda i: (i, 0)),
                   pl.BlockSpec((SUBLANES, LANES), lambda i: (0, 0))],
        out_shape=[jax.ShapeDtypeStruct((T, LANES), F32), jax.ShapeDtypeStruct((SUBLANES, LANES), F32)],
        scratch_shapes=[pltpu.VMEM((1, LANES), F32)],
        compiler_params=_cparams(("arbitrary",)), name=name)(x, rw_pad, rb_pad)
    return info, cnt


def _row_copy(src, dst, s, t, sem):
    return pltpu.make_async_copy(src.at[pl.ds(s, 1), :], dst.at[pl.ds(t, 1), :], sem)


DISPATCH_SLOTS = 3
ISSUE_UNROLL = 8


def _dispatch_kernel(dest_ref, x_hbm, xs_in, xs_out, buf, lsem, ssem, *, nblk):
    del xs_in
    i = pl.program_id(0)

    def load(blk):
        sl = blk % DISPATCH_SLOTS
        return pltpu.make_async_copy(x_hbm.at[pl.ds(blk * TOK_BLK, TOK_BLK), :], buf.at[sl], lsem.at[sl])

    def drain(blk):
        sl = blk % DISPATCH_SLOTS
        for a in range(2):
            pltpu.make_async_copy(buf.at[sl], xs_out.at[pl.ds(0, TOK_BLK), :], ssem.at[sl]).wait()

    @pl.when(i == 0)
    def _():
        load(0).start()
        if nblk > 1:
            load(1).start()

    slot = i % DISPATCH_SLOTS
    load(i).wait()

    def issue(k, carry):
        for a in range(2):
            _row_copy(buf.at[slot], xs_out, k, dest_ref[0, 0, 2 * k + a], ssem.at[slot]).start(priority=a)
        return carry

    lax.fori_loop(0, TOK_BLK, issue, 0, unroll=ISSUE_UNROLL)

    @pl.when(i > 0)
    def _():
        drain(i - 1)

    @pl.when(i + 2 < nblk)
    def _():
        load(i + 2).start()

    @pl.when(i == nblk - 1)
    def _():
        drain(i)


def _dispatch(x, dest, xs_zero, name):
    T, D = x.shape
    nblk = T // TOK_BLK
    dest3 = dest.reshape(nblk, 1, 2 * TOK_BLK)
    return pl.pallas_call(
        functools.partial(_dispatch_kernel, nblk=nblk), grid=(nblk,),
        in_specs=[pl.BlockSpec((1, 1, 2 * TOK_BLK), lambda i: (i, 0, 0), memory_space=pltpu.SMEM),
                  pl.BlockSpec(memory_space=pl.ANY), pl.BlockSpec(memory_space=pl.ANY)],
        out_specs=pl.BlockSpec(memory_space=pl.ANY),
        out_shape=jax.ShapeDtypeStruct(xs_zero.shape, F32),
        scratch_shapes=[pltpu.VMEM((DISPATCH_SLOTS, TOK_BLK, D), F32),
                        pltpu.SemaphoreType.DMA((DISPATCH_SLOTS,)),
                        pltpu.SemaphoreType.DMA((DISPATCH_SLOTS,))],
        input_output_aliases={2: 0},
        compiler_params=_cparams(("arbitrary",)), name=name)(dest3, x, xs_zero)


def _expert_kernel(be_ref, nu_ref, x_ref, wg_ref, wu_ref, wd_ref, o_ref, wg_bf, wu_bf, wd_bf):
    i = pl.program_id(0)
    active = i < nu_ref[0]

    @pl.when(jnp.logical_and(active, jnp.logical_or(i == 0, be_ref[i] != be_ref[jnp.maximum(i - 1, 0)])))
    def _():
        wg_bf[...] = wg_ref[0].astype(BF16)
        wu_bf[...] = wu_ref[0].astype(BF16)
        wd_bf[...] = wd_ref[0].astype(BF16)

    @pl.when(active)
    def _():
        x = x_ref[...].astype(BF16)
        g = jnp.dot(x, wg_bf[...], preferred_element_type=F32)
        u = jnp.dot(x, wu_bf[...], preferred_element_type=F32)
        hid = (g * _sigmoid(g)) * u
        o_ref[...] = jnp.dot(hid.astype(BF16), wd_bf[...], preferred_element_type=F32)

    @pl.when(i >= nu_ref[0])
    def _():
        o_ref[...] = jnp.zeros_like(o_ref)


def _experts(xs, blk_e, n_used, w_gate, w_up, w_down, layer, name):
    n_rows, D = xs.shape
    F = w_gate.shape[3]
    nb = n_rows // MOE_ROWS
    grid_spec = pltpu.PrefetchScalarGridSpec(
        num_scalar_prefetch=2, grid=(nb,),
        in_specs=[pl.BlockSpec((MOE_ROWS, D), lambda i, be, nu: (i, 0)),
                  pl.BlockSpec((None, 1, D, F), lambda i, be, nu: (layer, be[i], 0, 0)),
                  pl.BlockSpec((None, 1, D, F), lambda i, be, nu: (layer, be[i], 0, 0)),
                  pl.BlockSpec((None, 1, F, D), lambda i, be, nu: (layer, be[i], 0, 0))],
        out_specs=pl.BlockSpec((MOE_ROWS, D), lambda i, be, nu: (i, 0)),
        scratch_shapes=[pltpu.VMEM((D, F), BF16), pltpu.VMEM((D, F), BF16), pltpu.VMEM((F, D), BF16)])
    return pl.pallas_call(
        _expert_kernel, grid_spec=grid_spec,
        out_shape=jax.ShapeDtypeStruct((n_rows, D), F32),
        compiler_params=_cparams(("arbitrary",)), name=name)(blk_e, n_used, xs, w_gate, w_up, w_down)


def _combine_kernel(dcur_ref, dnext_ref, x_ref, info_ref, g_ref, b_ref, ys_hbm, o_ref, buf, sems, *, nblk):
    i = pl.program_id(0)
    slot = i % 2

    def issue(dref, sl):
        def body(k, carry):
            for a in range(2):
                _row_copy(ys_hbm, buf.at[sl, a], dref[0, 0, 2 * k + a], k, sems.at[sl]).start(priority=a)
            return carry
        lax.fori_loop(0, TOK_BLK, body, 0, unroll=ISSUE_UNROLL)

    @pl.when(i == 0)
    def _():
        issue(dcur_ref, 0)

    @pl.when(i + 1 < nblk)
    def _():
        issue(dnext_ref, 1 - slot)

    for a in range(2):
        pltpu.make_async_copy(ys_hbm.at[pl.ds(0, TOK_BLK), :], buf.at[slot, a], sems.at[slot]).wait()
    info = info_ref[...]
    y = info[:, 2:3] * buf[slot, 0] + info[:, 3:4] * buf[slot, 1]
    o_ref[...] = _layer_norm(DN_ALPHA * x_ref[...] + y, g_ref[...], b_ref[...])


def _combine(x, info, dest, ys, g, b, name):
    T, D = x.shape
    nblk = T // TOK_BLK
    dest3 = dest.reshape(nblk, 1, 2 * TOK_BLK)
    dspec = lambda f: pl.BlockSpec((1, 1, 2 * TOK_BLK), f, memory_space=pltpu.SMEM)
    return pl.pallas_call(
        functools.partial(_combine_kernel, nblk=nblk), grid=(nblk,),
        in_specs=[dspec(lambda i: (i, 0, 0)),
                  dspec(lambda i: (jnp.minimum(i + 1, nblk - 1), 0, 0)),
                  pl.BlockSpec((TOK_BLK, D), lambda i: (i, 0)),
                  pl.BlockSpec((TOK_BLK, LANES), lambda i: (i, 0)),
                  pl.BlockSpec((1, D), lambda i: (0, 0)), pl.BlockSpec((1, D), lambda i: (0, 0)),
                  pl.BlockSpec(memory_space=pl.ANY)],
        out_specs=pl.BlockSpec((TOK_BLK, D), lambda i: (i, 0)),
        out_shape=jax.ShapeDtypeStruct((T, D), F32),
        scratch_shapes=[pltpu.VMEM((2, 2, TOK_BLK, D), F32), pltpu.SemaphoreType.DMA((2,))],
        compiler_params=_cparams(("arbitrary",)), name=name)(dest3, dest3, x, info, g, b, ys)


def _moe_ln(x, rw_pad, rb_pad, w_gate, w_up, w_down, layer, g, b, tag):
    T, D = x.shape
    info, cnt = _router(x, rw_pad, rb_pad, f"router_{tag}")
    e = info[:, 0:2].astype(I32)
    rank = info[:, 4:6].astype(I32)
    counts = cnt[0, :N_EXPERTS].astype(I32)
    padded = (counts + MOE_ROWS - 1) // MOE_ROWS * MOE_ROWS
    pend = jnp.cumsum(padded)
    pstart = pend - padded
    dest = (pstart[e] + rank).reshape(-1)
    n_blocks = (2 * T + N_EXPERTS * (MOE_ROWS - 1) + MOE_ROWS - 1) // MOE_ROWS
    blk_start = jnp.arange(n_blocks, dtype=I32) * MOE_ROWS
    blk_e = jnp.minimum(jnp.sum((pend[None, :] <= blk_start[:, None]).astype(I32), axis=1), N_EXPERTS - 1)
    n_used = (pend[-1:] // MOE_ROWS).astype(I32)
    xs = _dispatch(x, dest, jnp.zeros((n_blocks * MOE_ROWS, D), F32), f"dispatch_{tag}")
    ys = _experts(xs, blk_e, n_used, w_gate, w_up, w_down, layer, f"experts_{tag}")
    return _combine(x, info, dest, ys, g, b, f"combine_{tag}")


def _tail_rows(a2d, B, S, n, col0, ncol):
    return jnp.stack([a2d[b * S + S - n:b * S + S, col0:col0 + ncol] for b in range(B)])


def _pad_tail(buf):
    return jnp.pad(buf, ((0, 0), (SUBLANES - (CONV_WIDTH - 1), 0), (0, 0)))


def _lru_layer(x, T_p, B, S, Bd, Ld, state_h, state_conv, p, ln_g, ln_b, tag):
    R = p["lam"].shape[1]
    proj = _proj(x, p["w_in"], p["b_in"], f"lru_in_{tag}")
    zeros_h = jnp.zeros((B, SUBLANES, R), F32)
    hg, h_p = _lru_seq(proj, None, zeros_h, zeros_h, p, row_off=0, B=B, Lp=S,
                       Lc=_pick(S, (256, 128, 64, 8)), L_valid=None, name=f"lru_seq_p_{tag}")
    h0 = jnp.broadcast_to(state_h[:, None, :], (Bd, SUBLANES, R))
    hg, h_s = _lru_seq(proj, hg, h0, _pad_tail(state_conv), p, row_off=T_p, B=Bd, Lp=SAMPLE_PAD,
                       Lc=SAMPLE_PAD, L_valid=Ld, name=f"lru_seq_s_{tag}")
    x_new = _outproj_ln(hg, p["w_out"], x, ln_g, ln_b, f"lru_out_{tag}")
    u_s = proj[T_p:, R:].reshape(Bd, SAMPLE_PAD, R)
    conv_p = _tail_rows(proj, B, S, CONV_WIDTH - 1, R, R)
    conv_s = jnp.concatenate([state_conv, u_s[:, :Ld]], axis=1)[:, -(CONV_WIDTH - 1):]
    return x_new, h_p, h_s, conv_p, conv_s


def _gdn_layer(x, T_p, B, S, Bd, Ld, state_s, state_conv, p, ln_g, ln_b, tag):
    QK, VD = GDN_HK * GDN_DK, GDN_HV * GDN_DV
    CD = 2 * QK + VD
    proj = _proj(x, p["w_main"], p["b_main"], f"gdn_in_{tag}")
    o, s_p = _gdn_seq(proj, None, jnp.zeros((B, GDN_HV, GDN_DK, GDN_DV), F32),
                      jnp.zeros((B, SUBLANES, CD), F32), p, row_off=0, B=B, Lp=S,
                      C=math.gcd(S, GDN_CHUNK), L_valid=None, name=f"gdn_seq_p_{tag}")
    o, s_s = _gdn_seq(proj, o, state_s, _pad_tail(state_conv), p, row_off=T_p, B=Bd, Lp=SAMPLE_PAD,
                      C=SAMPLE_PAD, L_valid=Ld, name=f"gdn_seq_s_{tag}")
    x_new = _outproj_ln(o, p["w_out"], x, ln_g, ln_b, f"gdn_out_{tag}")
    q_s = proj[T_p:, :CD].reshape(Bd, SAMPLE_PAD, CD)
    conv_p = _tail_rows(proj, B, S, CONV_WIDTH - 1, 0, CD)
    conv_s = jnp.concatenate([state_conv, q_s[:, :Ld]], axis=1)[:, -(CONV_WIDTH - 1):]
    return x_new, s_p, s_s, conv_p, conv_s


def _unpermute_tail(pg, B, S, d, keep, col0, ncol):
    tiles = -(-keep // SWA_TILE)
    rows = tiles * SWA_TILE
    t = _tail_rows(pg, B, S, rows, col0, ncol)
    t = t.reshape(B, tiles, d, SWA_TILE // d, ncol).transpose(0, 1, 3, 2, 4).reshape(B, rows, ncol)
    return t[:, rows - keep:]


def _swa_layer(x, T_p, B, S, Bd, Ld, caches, p, rope_cos, rope_ss, ln_g, ln_b, tag):
    W = SWA_HEADS * SWA_HD
    T_all = x.shape[0]
    n_s = T_all - T_p
    outs, lses, outs_s, lses_s, new_p, new_s = [], [], [], [], [], []
    for gi in range(SWA_GROUPS):
        d, window = SWA_DILATIONS[gi], SWA_WINDOWS[gi]
        pg = _swa_proj(x, p["w_in"][gi], rope_cos, rope_ss, d=d, tm=SWA_TILE, row_blk_off=0,
                       n_row_blks=T_p // SWA_TILE, name=f"swa_in_p{gi}_{tag}")
        o, l = _swa_attn(pg, B=B, S=S, d=d, name=f"swa_attn_p{gi}_{tag}")
        outs.append(o)
        lses.append(l)
        keep = min(window, S)
        kv = _unpermute_tail(pg, B, S, d, keep, W, 2 * W)
        new_p.append(kv.reshape(B, keep, 2, SWA_HEADS, SWA_HD))

        ps = _swa_proj(x, p["w_in"][gi], rope_cos, rope_ss, d=1, tm=n_s, row_blk_off=T_p // n_s,
                       n_row_blks=1, name=f"swa_in_s{gi}_{tag}")
        o_s, l_s = _swa_sample_attn(ps, caches[gi], d=d, window=window, name=f"swa_attn_s{gi}_{tag}")
        outs_s.append(o_s)
        lses_s.append(l_s)
        new_s.append(ps.reshape(Bd, SAMPLE_PAD, 3 * W)[:, :Ld, W:].reshape(Bd, Ld, 2, SWA_HEADS, SWA_HD))
    full = _swa_merge(outs, lses, rows=T_p, total_rows=T_all, name=f"swa_merge_p_{tag}")
    full = _swa_merge_sample(outs_s, lses_s, full, row_off=T_p, name=f"swa_merge_s_{tag}")
    x_new = _outproj_ln(full, p["w_out"], x, ln_g, ln_b, f"swa_out_{tag}")
    return x_new, new_p, new_s


def _rope_tables(T_p, S, n_s):
    half = SWA_HD // 2
    inv = ROPE_THETA ** (-2.0 * jnp.arange(half, dtype=F32) / SWA_HD)
    pos = jnp.concatenate([jnp.arange(T_p, dtype=I32) % S,
                           PAST_LEN + jnp.arange(n_s, dtype=I32) % SAMPLE_PAD]).astype(F32)
    ang = pos[:, None] * inv[None, :]
    cos, sin = jnp.cos(ang), jnp.sin(ang)
    return jnp.concatenate([cos, cos, cos, cos], 1), jnp.concatenate([-sin, sin, -sin, sin], 1)


def kernel(x_prompt, x_sample, state_lru_h, state_lru_conv, state_gdn_s, state_gdn_conv,
           cache_swa_w128, cache_swa_w512, cache_swa_w2048,
           lru_w_in, lru_b_in, lru_conv_w, lru_conv_b, lru_w_a, lru_b_a, lru_w_i, lru_b_i,
           lru_lambda, lru_w_out,
           gdn_w_in, gdn_conv_w, gdn_a_log, gdn_dt_bias, gdn_norm_w, gdn_w_out,
           swa_w_in, swa_w_out,
           ln1_g, ln1_b, ln2_g, ln2_b,
           router_w, router_b, moe_w_gate, moe_w_up, moe_w_down):
    B, S, D = x_prompt.shape
    Bd, Ld, _ = x_sample.shape
    T_p = B * S
    n_s = Bd * SAMPLE_PAD
    assert S % SWA_TILE == 0 and Ld <= SAMPLE_PAD and T_p % n_s == 0
    xs_pad = jnp.pad(x_sample, ((0, 0), (0, SAMPLE_PAD - Ld), (0, 0)))
    x = jnp.concatenate([x_prompt.reshape(T_p, D), xs_pad.reshape(n_s, D)], axis=0)

    row = lambda v: v.reshape(1, -1)
    pad8 = lambda w: jnp.pad(w, ((0, SUBLANES - w.shape[0]), (0, 0)))
    rw_pad = jnp.pad(router_w, ((0, 0), (0, LANES - N_EXPERTS)))
    rb_pad = jnp.concatenate([router_b, jnp.full((LANES - N_EXPERTS,), -1e30, F32)]).reshape(1, LANES)
    rope_cos, rope_ss = _rope_tables(T_p, S, n_s)
    caches = (cache_swa_w128, cache_swa_w512, cache_swa_w2048)

    res = {k: [] for k in ("lru_h_p", "lru_h_s", "lru_c_p", "lru_c_s", "gdn_s_p", "gdn_s_s", "gdn_c_p",
                           "gdn_c_s")}
    swa_p = [[] for _ in range(SWA_GROUPS)]
    swa_s = [[] for _ in range(SWA_GROUPS)]
    for i in range(DEPTH):
        kind, j = LAYER_KIND[i], KIND_INDEX[i]
        g1, b1 = row(ln1_g[i]), row(ln1_b[i])
        if kind == 0:
            p = dict(w_in=lru_w_in[j].astype(BF16), b_in=row(lru_b_in[j]), cw=pad8(lru_conv_w[j]),
                     cb=row(lru_conv_b[j]),
                     wai=jnp.concatenate([lru_w_a[j], lru_w_i[j]], axis=-1).astype(BF16),
                     ba=row(lru_b_a[j]), bi=row(lru_b_i[j]), lam=row(lru_lambda[j]),
                     w_out=lru_w_out[j].astype(BF16))
            x, h_p, h_s, c_p, c_s = _lru_layer(x, T_p, B, S, Bd, Ld, state_lru_h[j], state_lru_conv[j], p,
                                               g1, b1, f"l{i}")
            res["lru_h_p"].append(h_p)
            res["lru_h_s"].append(h_s)
            res["lru_c_p"].append(c_p)
            res["lru_c_s"].append(c_s)
        elif kind == 1:
            QK, VD = GDN_HK * GDN_DK, GDN_HV * GDN_DV
            CD = 2 * QK + VD
            w_in = gdn_w_in[j]
            p = dict(w_main=jnp.pad(w_in, ((0, 0), (0, LANES - 2 * GDN_HV))).astype(BF16),
                     b_main=jnp.zeros((1, CD + VD + LANES), F32),
                     cw=pad8(gdn_conv_w[j]),
                     alog=jnp.pad(gdn_a_log[j], (GDN_HV, LANES - 2 * GDN_HV)).reshape(1, LANES),
                     dtb=jnp.pad(gdn_dt_bias[j], (GDN_HV, LANES - 2 * GDN_HV)).reshape(1, LANES),
                     nw=row(gdn_norm_w[j]), w_out=gdn_w_out[j].astype(BF16))
            x, s_p, s_s, c_p, c_s = _gdn_layer(x, T_p, B, S, Bd, Ld, state_gdn_s[j], state_gdn_conv[j], p,
                                               g1, b1, f"l{i}")
            res["gdn_s_p"].append(s_p)
            res["gdn_s_s"].append(s_s)
            res["gdn_c_p"].append(c_p)
            res["gdn_c_s"].append(c_s)
        else:
            W = SWA_HEADS * SWA_HD
            w_in = swa_w_in[j].astype(BF16)
            p = dict(w_in=[w_in[:, gi * 3 * W:(gi + 1) * 3 * W] for gi in range(SWA_GROUPS)],
                     w_out=swa_w_out[j].astype(BF16))
            x, new_p, new_s = _swa_layer(x, T_p, B, S, Bd, Ld, tuple(c[j] for c in caches), p,
                                         rope_cos, rope_ss, g1, b1, f"l{i}")
            for gi in range(SWA_GROUPS):
                swa_p[gi].append(new_p[gi])
                swa_s[gi].append(new_s[gi])
        x = _moe_ln(x, rw_pad, rb_pad, moe_w_gate, moe_w_up, moe_w_down, i,
                    row(ln2_g[i]), row(ln2_b[i]), f"l{i}")

    y_prompt = x[:T_p].reshape(B, S, D)
    y_sample = x[T_p:].reshape(Bd, SAMPLE_PAD, D)[:, :Ld]
    st = lambda k: jnp.stack(res[k])
    return (y_prompt, y_sample, st("lru_h_p"), st("lru_h_s"), st("lru_c_p"), st("lru_c_s"),
            st("gdn_s_p"), st("gdn_s_s"), st("gdn_c_p"), st("gdn_c_s"),
            jnp.stack(swa_p[0]), jnp.stack(swa_s[0]), jnp.stack(swa_p[1]), jnp.stack(swa_s[1]),
            jnp.stack(swa_p[2]), jnp.stack(swa_s[2]))
```

```python
import functools
import math

import jax
import jax.numpy as jnp
from jax import lax
from jax.experimental import pallas as pl
from jax.experimental.pallas import tpu as pltpu

F32 = jnp.float32
BF16 = jnp.bfloat16
I32 = jnp.int32

DEPTH = 4
LAYER_KIND = (0, 1, 2, 0)
KIND_INDEX = (0, 0, 0, 1)
DN_ALPHA = (2 * DEPTH) ** 0.25
LN_EPS = 1e-5
NORM_EPS = 1e-6
CONV_WIDTH = 4
LRU_BLOCKS = 4
LRU_C = 8.0
GDN_HK = 8
GDN_HV = 16
GDN_DK = 128
GDN_DV = 128
GDN_CHUNK = 64
SWA_WINDOWS = (128, 512, 2048)
SWA_DILATIONS = (1, 4, 16)
SWA_GROUPS = 3
SWA_HEADS = 16
SWA_HD = 64
SWA_BLOCK = 128
ROPE_THETA = 10000.0
PAST_LEN = 8192
N_EXPERTS = 32
N_EXPERT_GROUPS = 4
EXPERTS_PER_GROUP = 8

LANES = 128
SUBLANES = 8
SAMPLE_PAD = 8
SWA_TILE = 2048
MOE_ROWS = 512
TOK_BLK = 256
VMEM_LIMIT = 56 * 1024 * 1024


def _cparams(sem):
    return pltpu.CompilerParams(dimension_semantics=sem, vmem_limit_bytes=VMEM_LIMIT)


def _pick(n, cands):
    for c in cands:
        if n % c == 0:
            return c
    raise ValueError(f"no tile for {n}")


def _bdot(a, b):
    return jnp.dot(a.astype(BF16), b.astype(BF16), preferred_element_type=F32)


def _nt(a, b):
    return lax.dot_general(a.astype(BF16), b.astype(BF16), (((1,), (1,)), ((), ())),
                           preferred_element_type=F32)


def _tn(a, b):
    return lax.dot_general(a.astype(BF16), b.astype(BF16), (((0,), (0,)), ((), ())),
                           preferred_element_type=F32)


def _split(a):
    hi = a.astype(BF16)
    return hi, (a - hi.astype(F32)).astype(BF16)


def _dot3(a, b):
    (ah, al), (bh, bl) = a, b
    return (jnp.dot(ah, bh, preferred_element_type=F32)
            + (jnp.dot(ah, bl, preferred_element_type=F32) + jnp.dot(al, bh, preferred_element_type=F32)))


def _sigmoid(x):
    return 1.0 / (1.0 + jnp.exp(-x))


def _softplus(x):
    return jnp.maximum(x, 0.0) + jnp.log(1.0 + jnp.exp(-jnp.abs(x)))


def _layer_norm(x, g, b):
    mu = jnp.mean(x, -1, keepdims=True)
    xc = x - mu
    var = jnp.mean(xc * xc, -1, keepdims=True)
    return xc * lax.rsqrt(var + LN_EPS) * g + b


def _row_scan_sum(x, n):
    rows = lax.broadcasted_iota(I32, x.shape, 0)
    s = 1
    while s < n:
        x = x + jnp.where(rows >= s, pltpu.roll(x, s, 0), 0.0)
        s *= 2
    return x


def _proj_kernel(x_ref, w_ref, b_ref, o_ref):
    o_ref[...] = jnp.dot(x_ref[...].astype(BF16), w_ref[...], preferred_element_type=F32) + b_ref[...]


def _proj(x, w_bf, b, name):
    M, K = x.shape
    N = w_bf.shape[1]
    tm = _pick(M, (1280, 1024, 512, 256, 128, 64, 8))
    tn = _pick(N, (1024, 896, 768, 512, 384, 256, 128))
    return pl.pallas_call(
        _proj_kernel, grid=(M // tm, N // tn),
        in_specs=[pl.BlockSpec((tm, K), lambda i, j: (i, 0)),
                  pl.BlockSpec((K, tn), lambda i, j: (0, j)),
                  pl.BlockSpec((1, tn), lambda i, j: (0, j))],
        out_specs=pl.BlockSpec((tm, tn), lambda i, j: (i, j)),
        out_shape=jax.ShapeDtypeStruct((M, N), F32),
        compiler_params=_cparams(("parallel", "parallel")), name=name)(x, w_bf, b)


def _outproj_ln_kernel(h_ref, w_ref, r_ref, g_ref, b_ref, o_ref):
    y = jnp.dot(h_ref[...].astype(BF16), w_ref[...], preferred_element_type=F32)
    o_ref[...] = _layer_norm(DN_ALPHA * r_ref[...] + y, g_ref[...], b_ref[...])


def _outproj_ln(h, w_bf, res, g, b, name):
    M, K = h.shape
    N = w_bf.shape[1]
    tm = _pick(M, (1280, 1024, 512, 256, 128, 64, 8))
    return pl.pallas_call(
        _outproj_ln_kernel, grid=(M // tm,),
        in_specs=[pl.BlockSpec((tm, K), lambda i: (i, 0)),
                  pl.BlockSpec((K, N), lambda i: (0, 0)),
                  pl.BlockSpec((tm, N), lambda i: (i, 0)),
                  pl.BlockSpec((1, N), lambda i: (0, 0)),
                  pl.BlockSpec((1, N), lambda i: (0, 0))],
        out_specs=pl.BlockSpec((tm, N), lambda i: (i, 0)),
        out_shape=jax.ShapeDtypeStruct((M, N), F32),
        compiler_params=_cparams(("parallel",)), name=name)(h, w_bf, res, g, b)


def _lru_kernel(gate_ref, u_ref, h0_ref, cb0_ref, cw_ref, cbias_ref, wai_ref, ba_ref, bi_ref,
                lam_ref, o_ref, hl_ref, ext, hcar, *, Lc, L_valid, R):
    c = pl.program_id(1)

    @pl.when(c == 0)
    def _():
        ext[0:SUBLANES, :] = cb0_ref[0]
        hcar[...] = h0_ref[0][0:1, :]

    @pl.when(c > 0)
    def _():
        ext[0:SUBLANES, :] = ext[Lc:Lc + SUBLANES, :]

    ext[SUBLANES:SUBLANES + Lc, :] = u_ref[...]
    u = cbias_ref[...] + cw_ref[3:4, :] * ext[8:8 + Lc, :]
    for k in range(CONV_WIDTH - 1):
        u = u + cw_ref[k:k + 1, :] * ext[5 + k:5 + k + Lc, :]

    bw = R // LRU_BLOCKS
    za, zi = [], []
    for n in range(LRU_BLOCKS):
        z = jnp.dot(u[:, n * bw:(n + 1) * bw].astype(BF16), wai_ref[n], preferred_element_type=F32)
        za.append(z[:, :bw])
        zi.append(z[:, bw:])
    r = _sigmoid(jnp.concatenate(za, axis=1) + ba_ref[...])
    ig = _sigmoid(jnp.concatenate(zi, axis=1) + bi_ref[...])
    log_a = (-LRU_C) * r * _softplus(-lam_ref[...])
    a = jnp.exp(log_a)
    th = jnp.tanh(log_a)
    xin = jnp.sqrt(-2.0 * th / (1.0 - th)) * ig * u

    rows = lax.broadcasted_iota(I32, (Lc, R), 0)
    if L_valid is not None:
        valid = (rows + c * Lc) < L_valid
        a = jnp.where(valid, a, 1.0)
        xin = jnp.where(valid, xin, 0.0)

    in_tile = rows % SUBLANES
    s = 1
    while s < SUBLANES:
        a_sh = jnp.where(in_tile >= s, pltpu.roll(a, s, 0), 1.0)
        b_sh = jnp.where(in_tile >= s, pltpu.roll(xin, s, 0), 0.0)
        xin = a * b_sh + xin
        a = a * a_sh
        s *= 2
    gel = jax.nn.gelu(gate_ref[...], approximate=True)
    h_prev = hcar[...]
    for t in range(Lc // SUBLANES):
        rs = slice(t * SUBLANES, (t + 1) * SUBLANES)
        h_t = a[rs, :] * h_prev + xin[rs, :]
        o_ref[rs, :] = h_t * gel[rs, :]
        h_prev = h_t[SUBLANES - 1:SUBLANES, :]
    hcar[...] = h_prev
    hl_ref[0] = jnp.broadcast_to(h_prev, (SUBLANES, R))


def _lru_seq(proj, prev_out, h0, cb0, p, *, row_off, B, Lp, Lc, L_valid, name):
    T_all, R2 = proj.shape
    R = R2 // 2
    nC = Lp // Lc
    off = row_off // Lc
    kern = functools.partial(_lru_kernel, Lc=Lc, L_valid=L_valid, R=R)
    row_map = lambda b, c: (off + b * nC + c, 0)
    const2 = lambda b, c: (0, 0)
    in_specs = [pl.BlockSpec((Lc, R), row_map),
                pl.BlockSpec((Lc, R), lambda b, c: (off + b * nC + c, 1)),
                pl.BlockSpec((1, SUBLANES, R), lambda b, c: (b, 0, 0)),
                pl.BlockSpec((1, SUBLANES, R), lambda b, c: (b, 0, 0)),
                pl.BlockSpec((SUBLANES, R), const2),
                pl.BlockSpec((1, R), const2),
                pl.BlockSpec((LRU_BLOCKS, R // LRU_BLOCKS, 2 * R // LRU_BLOCKS), lambda b, c: (0, 0, 0)),
                pl.BlockSpec((1, R), const2), pl.BlockSpec((1, R), const2), pl.BlockSpec((1, R), const2)]
    args = [proj, proj, h0, cb0, p["cw"], p["cb"], p["wai"], p["ba"], p["bi"], p["lam"]]
    aliases = {}
    if prev_out is not None:
        in_specs.append(pl.BlockSpec(memory_space=pl.ANY))
        args.append(prev_out)
        aliases = {len(args) - 1: 0}
        kern = _drop_extra_input(kern, n_in=10, n_extra=1)
    out, hl = pl.pallas_call(
        kern, grid=(B, nC), in_specs=in_specs,
        out_specs=[pl.BlockSpec((Lc, R), row_map), pl.BlockSpec((1, SUBLANES, R), lambda b, c: (b, 0, 0))],
        out_shape=[jax.ShapeDtypeStruct((T_all, R), F32), jax.ShapeDtypeStruct((B, SUBLANES, R), F32)],
        scratch_shapes=[pltpu.VMEM((Lc + SUBLANES, R), F32), pltpu.VMEM((1, R), F32)],
        input_output_aliases=aliases,
        compiler_params=_cparams(("parallel", "arbitrary")), name=name)(*args)
    return out, hl[:, 0, :]


def _drop_extra_input(kern, n_in, n_extra):
    def wrapped(*refs):
        return kern(*refs[:n_in], *refs[n_in + n_extra:])
    return wrapped


def _gdn_kernel(qkv_ref, z_ref, ba_ref, s0_ref, cb0_ref, cw_ref, alog_ref, dtb_ref, nw_ref,
                o_ref, sfin_ref, ext, S, *, C, L_valid, nC):
    c = pl.program_id(1)
    QK = GDN_HK * GDN_DK

    @pl.when(c == 0)
    def _():
        ext[0:SUBLANES, :] = cb0_ref[0]
        S[...] = s0_ref[0]

    @pl.when(c > 0)
    def _():
        ext[0:SUBLANES, :] = ext[C:C + SUBLANES, :]

    ext[SUBLANES:SUBLANES + C, :] = qkv_ref[...]
    xc = cw_ref[3:4, :] * ext[8:8 + C, :]
    for k in range(CONV_WIDTH - 1):
        xc = xc + cw_ref[k:k + 1, :] * ext[5 + k:5 + k + C, :]
    act = xc * _sigmoid(xc)

    ba = ba_ref[...]
    beta = _sigmoid(ba)
    g = pltpu.roll(-jnp.exp(alog_ref[...]) * _softplus(ba + dtb_ref[...]), LANES - GDN_HV, 1)
    if L_valid is not None:
        rows = lax.broadcasted_iota(I32, (C, LANES), 0)
        valid = (rows + c * C) < L_valid
        g = jnp.where(valid, g, 0.0)
        beta = jnp.where(valid, beta, 0.0)
    gc = _row_scan_sum(g, C)
    gct = jnp.transpose(gc)
    e_gc = jnp.exp(gc)
    g_last = gc[C - 1:C, :]
    e_rest = jnp.exp(g_last - gc)
    e_last = jnp.exp(g_last)

    ri = lax.broadcasted_iota(I32, (C, C), 0)
    ci = lax.broadcasted_iota(I32, (C, C), 1)
    tril = ri >= ci
    strict = ri > ci

    qn, kn = [], []
    for hk in range(GDN_HK):
        qh = act[:, hk * GDN_DK:(hk + 1) * GDN_DK]
        kh = act[:, QK + hk * GDN_DK:QK + (hk + 1) * GDN_DK]
        qn.append(qh * lax.rsqrt(jnp.sum(qh * qh, -1, keepdims=True) + NORM_EPS) * (GDN_DK ** -0.5))
        kn.append(kh * lax.rsqrt(jnp.sum(kh * kh, -1, keepdims=True) + NORM_EPS))

    rep = GDN_HV // GDN_HK
    heads = range(GDN_HV)
    qkk = [_nt(jnp.concatenate([qn[hk], kn[hk]], axis=0), kn[hk]) for hk in range(GDN_HK)]
    eye = jnp.where(ri == ci, 1.0, 0.0)
    decay, xs, ps, rhs = [], [], [], []
    for h in heads:
        kh = kn[h // rep]
        vh = act[:, 2 * QK + h * GDN_DV:2 * QK + (h + 1) * GDN_DV]
        bi = beta[:, h:h + 1]
        dec = jnp.exp(jnp.where(tril, gc[:, h:h + 1] - gct[h:h + 1, :], -jnp.inf))
        a_mat = jnp.where(strict, bi * qkk[h // rep][C:, :] * dec, 0.0)
        decay.append(dec)
        xs.append(eye - a_mat)
        ps.append(a_mat)
        rhs.append(jnp.concatenate([vh * bi, kh * (bi * e_gc[:, h:h + 1])], axis=1))
    p_sp = [_split(p) for p in ps]
    k = 2
    while k < C:
        ps = [_dot3(sp, sp) for sp in p_sp]
        p_sp = [_split(p) for p in ps]
        xs = [x + _dot3(_split(x), sp) for x, sp in zip(xs, p_sp)]
        k *= 2
    uw = [_bdot(x, r) for x, r in zip(xs, rhs)]
    ws = [_bdot(jnp.concatenate([uw[h][:, GDN_DV:], qn[h // rep] * e_gc[:, h:h + 1]], axis=0), S[h])
          for h in heads]
    v_new = [uw[h][:, :GDN_DV] - ws[h][:C, :] for h in heads]
    o_intra = [_bdot(qkk[h // rep][:C, :] * decay[h], v_new[h]) for h in heads]
    s_upd = [_tn(kn[h // rep] * e_rest[:, h:h + 1], v_new[h]) for h in heads]
    for h in heads:
        S[h] = S[h] * e_last[:, h:h + 1] + s_upd[h]
        o = ws[h][C:, :] + o_intra[h]
        o = o * lax.rsqrt(jnp.mean(o * o, -1, keepdims=True) + NORM_EPS) * nw_ref[...]
        zh = z_ref[:, h * GDN_DV:(h + 1) * GDN_DV]
        o_ref[:, h * GDN_DV:(h + 1) * GDN_DV] = o * (zh * _sigmoid(zh))

    @pl.when(c == nC - 1)
    def _():
        sfin_ref[0] = S[...]


def _gdn_seq(proj, prev_out, s0, cb0, p, *, row_off, B, Lp, C, L_valid, name):
    T_all = proj.shape[0]
    QK, VD = GDN_HK * GDN_DK, GDN_HV * GDN_DV
    CD = 2 * QK + VD
    nC = Lp // C
    off = row_off // C
    kern = functools.partial(_gdn_kernel, C=C, L_valid=L_valid, nC=nC)
    row_map = lambda b, c: (off + b * nC + c, 0)
    const2 = lambda b, c: (0, 0)
    zblk = CD // VD
    in_specs = [pl.BlockSpec((C, CD), row_map),
                pl.BlockSpec((C, VD), lambda b, c: (off + b * nC + c, zblk)),
                pl.BlockSpec((C, LANES), lambda b, c: (off + b * nC + c, (CD + VD) // LANES)),
                pl.BlockSpec((1, GDN_HV, GDN_DK, GDN_DV), lambda b, c: (b, 0, 0, 0)),
                pl.BlockSpec((1, SUBLANES, CD), lambda b, c: (b, 0, 0)),
                pl.BlockSpec((SUBLANES, CD), const2),
                pl.BlockSpec((1, LANES), const2), pl.BlockSpec((1, LANES), const2),
                pl.BlockSpec((1, GDN_DV), const2)]
    args = [proj, proj, proj, s0, cb0, p["cw"], p["alog"], p["dtb"], p["nw"]]
    aliases = {}
    if prev_out is not None:
        in_specs.append(pl.BlockSpec(memory_space=pl.ANY))
        args.append(prev_out)
        aliases = {len(args) - 1: 0}
        kern = _drop_extra_input(kern, n_in=9, n_extra=1)
    out, sfin = pl.pallas_call(
        kern, grid=(B, nC), in_specs=in_specs,
        out_specs=[pl.BlockSpec((C, VD), row_map),
                   pl.BlockSpec((1, GDN_HV, GDN_DK, GDN_DV), lambda b, c: (b, 0, 0, 0))],
        out_shape=[jax.ShapeDtypeStruct((T_all, VD), F32),
                   jax.ShapeDtypeStruct((B, GDN_HV, GDN_DK, GDN_DV), F32)],
        scratch_shapes=[pltpu.VMEM((C + SUBLANES, CD), F32), pltpu.VMEM((GDN_HV, GDN_DK, GDN_DV), F32)],
        input_output_aliases=aliases,
        compiler_params=_cparams(("parallel", "arbitrary")), name=name)(*args)
    return out, sfin


def _rope(x, cos, ss):
    W = x.shape[1]
    reps = W // LANES
    cfull = jnp.concatenate([cos] * reps, axis=1)
    sfull = jnp.concatenate([ss] * reps, axis=1)
    lane = lax.broadcasted_iota(I32, x.shape, 1)
    first = (lane % SWA_HD) < (SWA_HD // 2)
    partner = jnp.where(first, pltpu.roll(x, W - SWA_HD // 2, 1), pltpu.roll(x, SWA_HD // 2, 1))
    return x * cfull + partner * sfull


def _swa_proj_kernel(x_ref, w_ref, cos_ref, ss_ref, o_ref, acc_ref, *, d, tm, n_rot):
    j = pl.program_id(1)
    acc = jnp.dot(x_ref[...].astype(BF16), w_ref[...], preferred_element_type=F32)

    def emit(val):
        if d == 1:
            o_ref[...] = val
            return
        n = tm // d
        for cb in range(val.shape[1] // LANES):
            acc_ref[cb] = val[:, cb * LANES:(cb + 1) * LANES]
        for cb in range(val.shape[1] // LANES):
            for r in range(d):
                o_ref[r * n:(r + 1) * n, cb * LANES:(cb + 1) * LANES] = acc_ref[cb, pl.ds(r, n, stride=d), :]

    @pl.when(j < n_rot)
    def _():
        emit(_rope(acc, cos_ref[...], ss_ref[...]))

    @pl.when(j >= n_rot)
    def _():
        emit(acc)


def _swa_proj(x, w_bf, cos, ss, *, d, tm, row_blk_off, n_row_blks, name):
    K = x.shape[1]
    N = w_bf.shape[1]
    tn = 512
    n_rot = (2 * N // 3) // tn
    kern = functools.partial(_swa_proj_kernel, d=d, tm=tm, n_rot=n_rot)
    return pl.pallas_call(
        kern, grid=(n_row_blks, N // tn),
        in_specs=[pl.BlockSpec((tm, K), lambda i, j: (i + row_blk_off, 0)),
                  pl.BlockSpec((K, tn), lambda i, j: (0, j)),
                  pl.BlockSpec((tm, LANES), lambda i, j: (i + row_blk_off, 0)),
                  pl.BlockSpec((tm, LANES), lambda i, j: (i + row_blk_off, 0))],
        out_specs=pl.BlockSpec((tm, tn), lambda i, j: (i, j)),
        out_shape=jax.ShapeDtypeStruct((n_row_blks * tm, N), F32),
        scratch_shapes=[pltpu.VMEM((tn // LANES, tm, LANES), F32)],
        compiler_params=_cparams(("parallel", "parallel")), name=name)(x, w_bf, cos, ss)


def _swa_attn_kernel(q_ref, kc_ref, kp_ref, vc_ref, vp_ref, o_ref, l_ref, *, nbt):
    t = pl.program_id(1)
    j = pl.program_id(2)
    has_prev = jnp.logical_or(t > 0, (j % nbt) > 0)
    QB = SWA_BLOCK
    qi = lax.broadcasted_iota(I32, (QB, QB), 0)
    ki = lax.broadcasted_iota(I32, (QB, QB), 1)
    mask_c = ki <= qi
    mask_p = jnp.logical_and(ki >= qi, has_prev)
    sls = [slice(h * SWA_HD, (h + 1) * SWA_HD) for h in range(SWA_HEADS)]
    qs = [(q_ref[:, sl] * (SWA_HD ** -0.5)).astype(BF16) for sl in sls]
    scs = [_nt(q, kc_ref[:, sl]) for q, sl in zip(qs, sls)]
    sps = [_nt(q, kp_ref[:, sl]) for q, sl in zip(qs, sls)]
    pcs, pps, ms, ls = [], [], [], []
    for sc, sp in zip(scs, sps):
        sc = jnp.where(mask_c, sc, -jnp.inf)
        sp = jnp.where(mask_p, sp, -jnp.inf)
        m = jnp.maximum(jnp.max(sc, -1, keepdims=True), jnp.max(sp, -1, keepdims=True))
        pc = jnp.exp(sc - m)
        pp = jnp.exp(sp - m)
        l = jnp.sum(pc, -1, keepdims=True) + jnp.sum(pp, -1, keepdims=True)
        inv_l = 1.0 / l
        ls.append(l)
        ms.append(m)
        pcs.append((pc * inv_l).astype(BF16))
        pps.append((pp * inv_l).astype(BF16))
    ocs = [_bdot(pc, vc_ref[:, sl]) for pc, sl in zip(pcs, sls)]
    ops = [_bdot(pp, vp_ref[:, sl]) for pp, sl in zip(pps, sls)]
    for h, sl in enumerate(sls):
        o_ref[:, sl] = ocs[h] + ops[h]
        l_ref[:, sl] = jnp.broadcast_to(ms[h] + jnp.log(ls[h]), (QB, SWA_HD))


def _swa_attn(pg, *, B, S, d, name):
    W = SWA_HEADS * SWA_HD
    QB = SWA_BLOCK
    bpt = SWA_TILE // QB
    nbt = bpt // d
    nT = S // SWA_TILE
    nblk = S // QB

    def cur(col):
        return lambda b, t, j: (b * nblk + t * bpt + j, col)

    def prev(col):
        def f(b, t, j):
            first = (j % nbt) == 0
            pj = jnp.where(first, j + nbt - 1, j - 1)
            pt = jnp.where(first, jnp.maximum(t - 1, 0), t)
            return (b * nblk + pt * bpt + pj, col)
        return f

    blk = (QB, W)
    o, l = pl.pallas_call(
        functools.partial(_swa_attn_kernel, nbt=nbt), grid=(B, nT, bpt),
        in_specs=[pl.BlockSpec(blk, cur(0)), pl.BlockSpec(blk, cur(1)), pl.BlockSpec(blk, prev(1)),
                  pl.BlockSpec(blk, cur(2)), pl.BlockSpec(blk, prev(2))],
        out_specs=[pl.BlockSpec(blk, cur(0)), pl.BlockSpec(blk, cur(0))],
        out_shape=[jax.ShapeDtypeStruct((B * S, W), F32), jax.ShapeDtypeStruct((B * S, W), F32)],
        compiler_params=_cparams(("parallel", "parallel", "arbitrary")), name=name)(pg, pg, pg, pg, pg)
    return o, l


def _merge3(o0, l0, o1, l1, o2, l2):
    m = jnp.maximum(jnp.maximum(l0, l1), l2)
    w0 = jnp.exp(l0 - m)
    w1 = jnp.exp(l1 - m)
    w2 = jnp.exp(l2 - m)
    return (w0 * o0 + w1 * o1 + w2 * o2) / (w0 + w1 + w2)


def _swa_merge_kernel(o0, l0, o1, l1, o2, l2, out_ref, s1o, s1l, s2o, s2l, *, tm):
    for (src_o, src_l, dst_o, dst_l, d) in ((o1, l1, s1o, s1l, SWA_DILATIONS[1]),
                                            (o2, l2, s2o, s2l, SWA_DILATIONS[2])):
        n = tm // d
        for cb in range(dst_o.shape[0]):
            cs = slice(cb * LANES, (cb + 1) * LANES)
            for r in range(d):
                dst_o[cb, pl.ds(r, n, stride=d), :] = src_o[r * n:(r + 1) * n, cs]
                dst_l[cb, pl.ds(r, n, stride=d), :] = src_l[r * n:(r + 1) * n, cs]
    for cb in range(s1o.shape[0]):
        cs = slice(cb * LANES, (cb + 1) * LANES)
        out_ref[:, cs] = _merge3(o0[:, cs], l0[:, cs], s1o[cb], s1l[cb], s2o[cb], s2l[cb])


def _swa_merge(outs, lses, *, rows, total_rows, name):
    W = outs[0].shape[1]
    tm, tn = SWA_TILE, 256
    spec = pl.BlockSpec((tm, tn), lambda i, j: (i, j))
    args = [outs[0], lses[0], outs[1], lses[1], outs[2], lses[2]]
    return pl.pallas_call(
        functools.partial(_swa_merge_kernel, tm=tm), grid=(rows // tm, W // tn),
        in_specs=[spec] * 6, out_specs=spec,
        out_shape=jax.ShapeDtypeStruct((total_rows, W), F32),
        scratch_shapes=[pltpu.VMEM((tn // LANES, tm, LANES), F32)] * 4,
        compiler_params=_cparams(("parallel", "parallel")), name=name)(*args)


def _swa_sample_kernel(x_ref, c_ref, o_ref, l_ref, *, d, window, Wb):
    L = SAMPLE_PAD
    W = SWA_HEADS * SWA_HD
    ccol = lax.broadcasted_iota(I32, (L, Wb), 1)
    lrow = lax.broadcasted_iota(I32, (L, Wb), 0)
    dist = Wb + lrow - ccol
    mask_c = jnp.logical_and(dist % d == 0, dist <= window)
    ln = lax.broadcasted_iota(I32, (L, L), 0)
    lk = lax.broadcasted_iota(I32, (L, L), 1)
    dn = ln - lk
    mask_n = jnp.logical_and(jnp.logical_and(dn >= 0, dn % d == 0), dn <= window)
    heads = range(SWA_HEADS)
    sls = [slice(h * SWA_HD, (h + 1) * SWA_HD) for h in heads]
    qs = [(x_ref[:, sl] * (SWA_HD ** -0.5)).astype(BF16) for sl in sls]
    scs = [_bdot(qs[h], c_ref[0, h]) for h in heads]
    sns = [_nt(qs[h], x_ref[:, W + h * SWA_HD:W + (h + 1) * SWA_HD]) for h in heads]
    pcs, pns, ms, ls = [], [], [], []
    for sc, sn in zip(scs, sns):
        sc = jnp.where(mask_c, sc, -jnp.inf)
        sn = jnp.where(mask_n, sn, -jnp.inf)
        m = jnp.maximum(jnp.max(sc, -1, keepdims=True), jnp.max(sn, -1, keepdims=True))
        pc = jnp.exp(sc - m)
        pn = jnp.exp(sn - m)
        l = jnp.sum(pc, -1, keepdims=True) + jnp.sum(pn, -1, keepdims=True)
        inv_l = 1.0 / l
        ls.append(l)
        ms.append(m)
        pcs.append((pc * inv_l).astype(BF16))
        pns.append((pn * inv_l).astype(BF16))
    ocs = [_nt(pcs[h], c_ref[1, h]) for h in heads]
    ons = [_bdot(pns[h], x_ref[:, 2 * W + h * SWA_HD:2 * W + (h + 1) * SWA_HD]) for h in heads]
    for h, sl in enumerate(sls):
        o_ref[:, sl] = ocs[h] + ons[h]
        l_ref[:, sl] = jnp.broadcast_to(ms[h] + jnp.log(ls[h]), (L, SWA_HD))


def _swa_sample_attn(ps, cache, *, d, window, name):
    Bd, Wb = cache.shape[0], cache.shape[1]
    H, hd = SWA_HEADS, SWA_HD
    W = H * hd
    L = SAMPLE_PAD
    assert Wb % d == 0
    ct = jnp.transpose(cache, (0, 2, 3, 4, 1))
    kern = functools.partial(_swa_sample_kernel, d=d, window=window, Wb=Wb)
    oblk = pl.BlockSpec((L, W), lambda b: (b, 0))
    o, l = pl.pallas_call(
        kern, grid=(Bd,),
        in_specs=[pl.BlockSpec((L, 3 * W), lambda b: (b, 0)),
                  pl.BlockSpec((None, 2, H, hd, Wb), lambda b: (b, 0, 0, 0, 0))],
        out_specs=[oblk, oblk],
        out_shape=[jax.ShapeDtypeStruct((Bd * L, W), F32)] * 2,
        compiler_params=_cparams(("parallel",)), name=name)(ps, ct)
    return o, l


def _merge_plain_kernel(o0, l0, o1, l1, o2, l2, out_ref):
    out_ref[...] = _merge3(o0[...], l0[...], o1[...], l1[...], o2[...], l2[...])


def _swa_merge_sample(outs, lses, prev_out, *, row_off, name):
    rows, W = outs[0].shape
    T_all = prev_out.shape[0]
    off = row_off // rows
    spec = pl.BlockSpec((rows, W), lambda i: (0, 0))
    args = [outs[0], lses[0], outs[1], lses[1], outs[2], lses[2], prev_out]
    kern = _drop_extra_input(_merge_plain_kernel, n_in=6, n_extra=1)
    return pl.pallas_call(
        kern, grid=(1,),
        in_specs=[spec] * 6 + [pl.BlockSpec(memory_space=pl.ANY)],
        out_specs=pl.BlockSpec((rows, W), lambda i: (off, 0)),
        out_shape=jax.ShapeDtypeStruct((T_all, W), F32),
        input_output_aliases={6: 0},
        compiler_params=_cparams(("arbitrary",)), name=name)(*args)


def _router_kernel(x_ref, w_ref, b_ref, info_ref, cnt_ref, car, *, tm):
    i = pl.program_id(0)

    @pl.when(i == 0)
    def _():
        car[...] = jnp.zeros_like(car)

    logits = jnp.dot(x_ref[...], w_ref[...], precision=lax.Precision.HIGHEST,
                     preferred_element_type=F32) + b_ref[...]
    m = jnp.max(logits, -1, keepdims=True)
    e = jnp.exp(logits - m)
    probs = e / jnp.sum(e, -1, keepdims=True)
    lane = lax.broadcasted_iota(I32, (tm, LANES), 1).astype(F32)
    BIG = 4.0 * LANES

    best = None
    for gi in range(N_EXPERT_GROUPS):
        lo = float(gi * EXPERTS_PER_GROUP)
        masked = jnp.where(jnp.logical_and(lane >= lo, lane < lo + EXPERTS_PER_GROUP), probs, -1.0)
        m1 = jnp.max(masked, -1, keepdims=True)
        i1 = jnp.min(jnp.where(masked == m1, lane, BIG), -1, keepdims=True)
        masked2 = jnp.where(lane == i1, -1.0, masked)
        m2 = jnp.max(masked2, -1, keepdims=True)
        i2 = jnp.min(jnp.where(masked2 == m2, lane, BIG), -1, keepdims=True)
        score = m1 + m2
        if best is None:
            best = (score, m1, i1, m2, i2)
        else:
            take = score > best[0]
            best = tuple(jnp.where(take, n, o) for n, o in zip((score, m1, i1, m2, i2), best))
    _, m1, i1, m2, i2 = best
    wsum = m1 + m2
    w1 = m1 / wsum
    w2 = m2 / wsum

    onehot = jnp.where(jnp.logical_or(lane == i1, lane == i2), 1.0, 0.0)
    ri = lax.broadcasted_iota(I32, (tm, tm), 0)
    ci = lax.broadcasted_iota(I32, (tm, tm), 1)
    lower = jnp.where(ri > ci, 1.0, 0.0)
    before = _bdot(lower, onehot) + car[...]
    r1 = jnp.sum(jnp.where(lane == i1, before, 0.0), -1, keepdims=True)
    r2 = jnp.sum(jnp.where(lane == i2, before, 0.0), -1, keepdims=True)
    car[...] = car[...] + jnp.sum(onehot, 0, keepdims=True)
    cnt_ref[...] = jnp.broadcast_to(car[...], cnt_ref.shape)

    info = jnp.zeros((tm, LANES), F32)
    for k, val in enumerate((i1, i2, w1, w2, r1, r2)):
        info = jnp.where(lane == float(k), val, info)
    info_ref[...] = info


def _router(x, rw_pad, rb_pad, name):
    T, D = x.shape
    tm = _pick(T, (256, 128, 64, 8))
    info, cnt = pl.pallas_call(
        functools.partial(_router_kernel, tm=tm), grid=(T // tm,),
        in_specs=[pl.BlockSpec((tm, D), lambda i: (i, 0)),
                  pl.BlockSpec((D, LANES), lambda i: (0, 0)),
                  pl.BlockSpec((1, LANES), lambda i: (0, 0))],
        out_specs=[pl.BlockSpec((tm, LANES), lambda i: (i, 0)),
                   pl.BlockSpec((SUBLANES, LANES), lambda i: (0, 0))],
        out_shape=[jax.ShapeDtypeStruct((T, LANES), F32), jax.ShapeDtypeStruct((SUBLANES, LANES), F32)],
        scratch_shapes=[pltpu.VMEM((1, LANES), F32)],
        compiler_params=_cparams(("arbitrary",)), name=name)(x, rw_pad, rb_pad)
    return info, cnt


def _row_copy(src, dst, s, t, sem):
    return pltpu.make_async_copy(src.at[pl.ds(s, 1), :], dst.at[pl.ds(t, 1), :], sem)


DISPATCH_SLOTS = 3
ISSUE_UNROLL = 8


def _dispatch_kernel(dest_ref, x_hbm, xs_in, xs_out, buf, lsem, ssem, *, nblk):
    del xs_in
    i = pl.program_id(0)

    def load(blk):
        sl = blk % DISPATCH_SLOTS
        return pltpu.make_async_copy(x_hbm.at[pl.ds(blk * TOK_BLK, TOK_BLK), :], buf.at[sl], lsem.at[sl])

    def drain(blk):
        sl = blk % DISPATCH_SLOTS
        for a in range(2):
            pltpu.make_async_copy(buf.at[sl], xs_out.at[pl.ds(0, TOK_BLK), :], ssem.at[sl]).wait()

    @pl.when(i == 0)
    def _():
        load(0).start()
        if nblk > 1:
            load(1).start()

    slot = i % DISPATCH_SLOTS
    load(i).wait()

    def issue(k, carry):
        for a in range(2):
            _row_copy(buf.at[slot], xs_out, k, dest_ref[0, 0, 2 * k + a], ssem.at[slot]).start(priority=a)
        return carry

    lax.fori_loop(0, TOK_BLK, issue, 0, unroll=ISSUE_UNROLL)

    @pl.when(i > 0)
    def _():
        drain(i - 1)

    @pl.when(i + 2 < nblk)
    def _():
        load(i + 2).start()

    @pl.when(i == nblk - 1)
    def _():
        drain(i)


def _dispatch(x, dest, xs_zero, name):
    T, D = x.shape
    nblk = T // TOK_BLK
    dest3 = dest.reshape(nblk, 1, 2 * TOK_BLK)
    return pl.pallas_call(
        functools.partial(_dispatch_kernel, nblk=nblk), grid=(nblk,),
        in_specs=[pl.BlockSpec((1, 1, 2 * TOK_BLK), lambda i: (i, 0, 0), memory_space=pltpu.SMEM),
                  pl.BlockSpec(memory_space=pl.ANY), pl.BlockSpec(memory_space=pl.ANY)],
        out_specs=pl.BlockSpec(memory_space=pl.ANY),
        out_shape=jax.ShapeDtypeStruct(xs_zero.shape, F32),
        scratch_shapes=[pltpu.VMEM((DISPATCH_SLOTS, TOK_BLK, D), F32),
                        pltpu.SemaphoreType.DMA((DISPATCH_SLOTS,)),
                        pltpu.SemaphoreType.DMA((DISPATCH_SLOTS,))],
        input_output_aliases={2: 0},
        compiler_params=_cparams(("arbitrary",)), name=name)(dest3, x, xs_zero)


def _expert_kernel(be_ref, nu_ref, x_ref, wg_ref, wu_ref, wd_ref, o_ref, wg_bf, wu_bf, wd_bf):
    i = pl.program_id(0)
    active = i < nu_ref[0]

    @pl.when(jnp.logical_and(active, jnp.logical_or(i == 0, be_ref[i] != be_ref[jnp.maximum(i - 1, 0)])))
    def _():
        wg_bf[...] = wg_ref[0].astype(BF16)
        wu_bf[...] = wu_ref[0].astype(BF16)
        wd_bf[...] = wd_ref[0].astype(BF16)

    @pl.when(active)
    def _():
        x = x_ref[...].astype(BF16)
        g = jnp.dot(x, wg_bf[...], preferred_element_type=F32)
        u = jnp.dot(x, wu_bf[...], preferred_element_type=F32)
        hid = (g * _sigmoid(g)) * u
        o_ref[...] = jnp.dot(hid.astype(BF16), wd_bf[...], preferred_element_type=F32)

    @pl.when(i >= nu_ref[0])
    def _():
        o_ref[...] = jnp.zeros_like(o_ref)


def _experts(xs, blk_e, n_used, w_gate, w_up, w_down, layer, name):
    n_rows, D = xs.shape
    F = w_gate.shape[3]
    nb = n_rows // MOE_ROWS
    grid_spec = pltpu.PrefetchScalarGridSpec(
        num_scalar_prefetch=2, grid=(nb,),
        in_specs=[pl.BlockSpec((MOE_ROWS, D), lambda i, be, nu: (i, 0)),
                  pl.BlockSpec((None, 1, D, F), lambda i, be, nu: (layer, be[i], 0, 0)),
                  pl.BlockSpec((None, 1, D, F), lambda i, be, nu: (layer, be[i], 0, 0)),
                  pl.BlockSpec((None, 1, F, D), lambda i, be, nu: (layer, be[i], 0, 0))],
        out_specs=pl.BlockSpec((MOE_ROWS, D), lambda i, be, nu: (i, 0)),
        scratch_shapes=[pltpu.VMEM((D, F), BF16), pltpu.VMEM((D, F), BF16), pltpu.VMEM((F, D), BF16)])
    return pl.pallas_call(
        _expert_kernel, grid_spec=grid_spec,
        out_shape=jax.ShapeDtypeStruct((n_rows, D), F32),
        compiler_params=_cparams(("arbitrary",)), name=name)(blk_e, n_used, xs, w_gate, w_up, w_down)


def _combine_kernel(dcur_ref, dnext_ref, x_ref, info_ref, g_ref, b_ref, ys_hbm, o_ref, buf, sems, *, nblk):
    i = pl.program_id(0)
    slot = i % 2

    def issue(dref, sl):
        def body(k, carry):
            for a in range(2):
                _row_copy(ys_hbm, buf.at[sl, a], dref[0, 0, 2 * k + a], k, sems.at[sl]).start(priority=a)
            return carry
        lax.fori_loop(0, TOK_BLK, body, 0, unroll=ISSUE_UNROLL)

    @pl.when(i == 0)
    def _():
        issue(dcur_ref, 0)

    @pl.when(i + 1 < nblk)
    def _():
        issue(dnext_ref, 1 - slot)

    for a in range(2):
        pltpu.make_async_copy(ys_hbm.at[pl.ds(0, TOK_BLK), :], buf.at[slot, a], sems.at[slot]).wait()
    info = info_ref[...]
    y = info[:, 2:3] * buf[slot, 0] + info[:, 3:4] * buf[slot, 1]
    o_ref[...] = _layer_norm(DN_ALPHA * x_ref[...] + y, g_ref[...], b_ref[...])


def _combine(x, info, dest, ys, g, b, name):
    T, D = x.shape
    nblk = T // TOK_BLK
    dest3 = dest.reshape(nblk, 1, 2 * TOK_BLK)
    dspec = lambda f: pl.BlockSpec((1, 1, 2 * TOK_BLK), f, memory_space=pltpu.SMEM)
    return pl.pallas_call(
        functools.partial(_combine_kernel, nblk=nblk), grid=(nblk,),
        in_specs=[dspec(lambda i: (i, 0, 0)),
                  dspec(lambda i: (jnp.minimum(i + 1, nblk - 1), 0, 0)),
                  pl.BlockSpec((TOK_BLK, D), lambda i: (i, 0)),
                  pl.BlockSpec((TOK_BLK, LANES), lambda i: (i, 0)),
                  pl.BlockSpec((1, D), lambda i: (0, 0)), pl.BlockSpec((1, D), lambda i: (0, 0)),
                  pl.BlockSpec(memory_space=pl.ANY)],
        out_specs=pl.BlockSpec((TOK_BLK, D), lambda i: (i, 0)),
        out_shape=jax.ShapeDtypeStruct((T, D), F32),
        scratch_shapes=[pltpu.VMEM((2, 2, TOK_BLK, D), F32), pltpu.SemaphoreType.DMA((2,))],
        compiler_params=_cparams(("arbitrary",)), name=name)(dest3, dest3, x, info, g, b, ys)


def _moe_ln(x, rw_pad, rb_pad, w_gate, w_up, w_down, layer, g, b, tag):
    T, D = x.shape
    info, cnt = _router(x, rw_pad, rb_pad, f"router_{tag}")
    e = info[:, 0:2].astype(I32)
    rank = info[:, 4:6].astype(I32)
    counts = cnt[0, :N_EXPERTS].astype(I32)
    padded = (counts + MOE_ROWS - 1) // MOE_ROWS * MOE_ROWS
    pend = jnp.cumsum(padded)
    pstart = pend - padded
    dest = (pstart[e] + rank).reshape(-1)
    n_blocks = (2 * T + N_EXPERTS * (MOE_ROWS - 1) + MOE_ROWS - 1) // MOE_ROWS
    blk_start = jnp.arange(n_blocks, dtype=I32) * MOE_ROWS
    blk_e = jnp.minimum(jnp.sum((pend[None, :] <= blk_start[:, None]).astype(I32), axis=1), N_EXPERTS - 1)
    n_used = (pend[-1:] // MOE_ROWS).astype(I32)
    xs = _dispatch(x, dest, jnp.zeros((n_blocks * MOE_ROWS, D), F32), f"dispatch_{tag}")
    ys = _experts(xs, blk_e, n_used, w_gate, w_up, w_down, layer, f"experts_{tag}")
    return _combine(x, info, dest, ys, g, b, f"combine_{tag}")


def _tail_rows(a2d, B, S, n, col0, ncol):
    return jnp.stack([a2d[b * S + S - n:b * S + S, col0:col0 + ncol] for b in range(B)])


def _pad_tail(buf):
    return jnp.pad(buf, ((0, 0), (SUBLANES - (CONV_WIDTH - 1), 0), (0, 0)))


def _lru_layer(x, T_p, B, S, Bd, Ld, state_h, state_conv, p, ln_g, ln_b, tag):
    R = p["lam"].shape[1]
    proj = _proj(x, p["w_in"], p["b_in"], f"lru_in_{tag}")
    zeros_h = jnp.zeros((B, SUBLANES, R), F32)
    hg, h_p = _lru_seq(proj, None, zeros_h, zeros_h, p, row_off=0, B=B, Lp=S,
                       Lc=_pick(S, (256, 128, 64, 8)), L_valid=None, name=f"lru_seq_p_{tag}")
    h0 = jnp.broadcast_to(state_h[:, None, :], (Bd, SUBLANES, R))
    hg, h_s = _lru_seq(proj, hg, h0, _pad_tail(state_conv), p, row_off=T_p, B=Bd, Lp=SAMPLE_PAD,
                       Lc=SAMPLE_PAD, L_valid=Ld, name=f"lru_seq_s_{tag}")
    x_new = _outproj_ln(hg, p["w_out"], x, ln_g, ln_b, f"lru_out_{tag}")
    u_s = proj[T_p:, R:].reshape(Bd, SAMPLE_PAD, R)
    conv_p = _tail_rows(proj, B, S, CONV_WIDTH - 1, R, R)
    conv_s = jnp.concatenate([state_conv, u_s[:, :Ld]], axis=1)[:, -(CONV_WIDTH - 1):]
    return x_new, h_p, h_s, conv_p, conv_s


def _gdn_layer(x, T_p, B, S, Bd, Ld, state_s, state_conv, p, ln_g, ln_b, tag):
    QK, VD = GDN_HK * GDN_DK, GDN_HV * GDN_DV
    CD = 2 * QK + VD
    proj = _proj(x, p["w_main"], p["b_main"], f"gdn_in_{tag}")
    o, s_p = _gdn_seq(proj, None, jnp.zeros((B, GDN_HV, GDN_DK, GDN_DV), F32),
                      jnp.zeros((B, SUBLANES, CD), F32), p, row_off=0, B=B, Lp=S,
                      C=math.gcd(S, GDN_CHUNK), L_valid=None, name=f"gdn_seq_p_{tag}")
    o, s_s = _gdn_seq(proj, o, state_s, _pad_tail(state_conv), p, row_off=T_p, B=Bd, Lp=SAMPLE_PAD,
                      C=SAMPLE_PAD, L_valid=Ld, name=f"gdn_seq_s_{tag}")
    x_new = _outproj_ln(o, p["w_out"], x, ln_g, ln_b, f"gdn_out_{tag}")
    q_s = proj[T_p:, :CD].reshape(Bd, SAMPLE_PAD, CD)
    conv_p = _tail_rows(proj, B, S, CONV_WIDTH - 1, 0, CD)
    conv_s = jnp.concatenate([state_conv, q_s[:, :Ld]], axis=1)[:, -(CONV_WIDTH - 1):]
    return x_new, s_p, s_s, conv_p, conv_s


def _unpermute_tail(pg, B, S, d, keep, col0, ncol):
    tiles = -(-keep // SWA_TILE)
    rows = tiles * SWA_TILE
    t = _tail_rows(pg, B, S, rows, col0, ncol)
    t = t.reshape(B, tiles, d, SWA_TILE // d, ncol).transpose(0, 1, 3, 2, 4).reshape(B, rows, ncol)
    return t[:, rows - keep:]


def _swa_layer(x, T_p, B, S, Bd, Ld, caches, p, rope_cos, rope_ss, ln_g, ln_b, tag):
    W = SWA_HEADS * SWA_HD
    T_all = x.shape[0]
    n_s = T_all - T_p
    outs, lses, outs_s, lses_s, new_p, new_s = [], [], [], [], [], []
    for gi in range(SWA_GROUPS):
        d, window = SWA_DILATIONS[gi], SWA_WINDOWS[gi]
        pg = _swa_proj(x, p["w_in"][gi], rope_cos, rope_ss, d=d, tm=SWA_TILE, row_blk_off=0,
                       n_row_blks=T_p // SWA_TILE, name=f"swa_in_p{gi}_{tag}")
        o, l = _swa_attn(pg, B=B, S=S, d=d, name=f"swa_attn_p{gi}_{tag}")
        outs.append(o)
        lses.append(l)
        keep = min(window, S)
        kv = _unpermute_tail(pg, B, S, d, keep, W, 2 * W)
        new_p.append(kv.reshape(B, keep, 2, SWA_HEADS, SWA_HD))

        ps = _swa_proj(x, p["w_in"][gi], rope_cos, rope_ss, d=1, tm=n_s, row_blk_off=T_p // n_s,
                       n_row_blks=1, name=f"swa_in_s{gi}_{tag}")
        o_s, l_s = _swa_sample_attn(ps, caches[gi], d=d, window=window, name=f"swa_attn_s{gi}_{tag}")
        outs_s.append(o_s)
        lses_s.append(l_s)
        new_s.append(ps.reshape(Bd, SAMPLE_PAD, 3 * W)[:, :Ld, W:].reshape(Bd, Ld, 2, SWA_HEADS, SWA_HD))
    full = _swa_merge(outs, lses, rows=T_p, total_rows=T_all, name=f"swa_merge_p_{tag}")
    full = _swa_merge_sample(outs_s, lses_s, full, row_off=T_p, name=f"swa_merge_s_{tag}")
    x_new = _outproj_ln(full, p["w_out"], x, ln_g, ln_b, f"swa_out_{tag}")
    return x_new, new_p, new_s


def _rope_tables(T_p, S, n_s):
    half = SWA_HD // 2
    inv = ROPE_THETA ** (-2.0 * jnp.arange(half, dtype=F32) / SWA_HD)
    pos = jnp.concatenate([jnp.arange(T_p, dtype=I32) % S,
                           PAST_LEN + jnp.arange(n_s, dtype=I32) % SAMPLE_PAD]).astype(F32)
    ang = pos[:, None] * inv[None, :]
    cos, sin = jnp.cos(ang), jnp.sin(ang)
    return jnp.concatenate([cos, cos, cos, cos], 1), jnp.concatenate([-sin, sin, -sin, sin], 1)


def kernel(x_prompt, x_sample, state_lru_h, state_lru_conv, state_gdn_s, state_gdn_conv,
           cache_swa_w128, cache_swa_w512, cache_swa_w2048,
           lru_w_in, lru_b_in, lru_conv_w, lru_conv_b, lru_w_a, lru_b_a, lru_w_i, lru_b_i,
           lru_lambda, lru_w_out,
           gdn_w_in, gdn_conv_w, gdn_a_log, gdn_dt_bias, gdn_norm_w, gdn_w_out,
           swa_w_in, swa_w_out,
           ln1_g, ln1_b, ln2_g, ln2_b,
           router_w, router_b, moe_w_gate, moe_w_up, moe_w_down):
    B, S, D = x_prompt.shape
    Bd, Ld, _ = x_sample.shape
    T_p = B * S
    n_s = Bd * SAMPLE_PAD
    assert S % SWA_TILE == 0 and Ld <= SAMPLE_PAD and T_p % n_s == 0
    xs_pad = jnp.pad(x_sample, ((0, 0), (0, SAMPLE_PAD - Ld), (0, 0)))
    x = jnp.concatenate([x_prompt.reshape(T_p, D), xs_pad.reshape(n_s, D)], axis=0)

    row = lambda v: v.reshape(1, -1)
    pad8 = lambda w: jnp.pad(w, ((0, SUBLANES - w.shape[0]), (0, 0)))
    rw_pad = jnp.pad(router_w, ((0, 0), (0, LANES - N_EXPERTS)))
    rb_pad = jnp.concatenate([router_b, jnp.full((LANES - N_EXPERTS,), -1e30, F32)]).reshape(1, LANES)
    rope_cos, rope_ss = _rope_tables(T_p, S, n_s)
    caches = (cache_swa_w128, cache_swa_w512, cache_swa_w2048)

    res = {k: [] for k in ("lru_h_p", "lru_h_s", "lru_c_p", "lru_c_s", "gdn_s_p", "gdn_s_s", "gdn_c_p",
                           "gdn_c_s")}
    swa_p = [[] for _ in range(SWA_GROUPS)]
    swa_s = [[] for _ in range(SWA_GROUPS)]
    for i in range(DEPTH):
        kind, j = LAYER_KIND[i], KIND_INDEX[i]
        g1, b1 = row(ln1_g[i]), row(ln1_b[i])
        if kind == 0:
            p = dict(w_in=lru_w_in[j].astype(BF16), b_in=row(lru_b_in[j]), cw=pad8(lru_conv_w[j]),
                     cb=row(lru_conv_b[j]),
                     wai=jnp.concatenate([lru_w_a[j], lru_w_i[j]], axis=-1).astype(BF16),
                     ba=row(lru_b_a[j]), bi=row(lru_b_i[j]), lam=row(lru_lambda[j]),
                     w_out=lru_w_out[j].astype(BF16))
            x, h_p, h_s, c_p, c_s = _lru_layer(x, T_p, B, S, Bd, Ld, state_lru_h[j], state_lru_conv[j], p,
                                               g1, b1, f"l{i}")
            res["lru_h_p"].append(h_p)
            res["lru_h_s"].append(h_s)
            res["lru_c_p"].append(c_p)
            res["lru_c_s"].append(c_s)
        elif kind == 1:
            QK, VD = GDN_HK * GDN_DK, GDN_HV * GDN_DV
            CD = 2 * QK + VD
            w_in = gdn_w_in[j]
            p = dict(w_main=jnp.pad(w_in, ((0, 0), (0, LANES - 2 * GDN_HV))).astype(BF16),
                     b_main=jnp.zeros((1, CD + VD + LANES), F32),
                     cw=pad8(gdn_conv_w[j]),
                     alog=jnp.pad(gdn_a_log[j], (GDN_HV, LANES - 2 * GDN_HV)).reshape(1, LANES),
                     dtb=jnp.pad(gdn_dt_bias[j], (GDN_HV, LANES - 2 * GDN_HV)).reshape(1, LANES),
                     nw=row(gdn_norm_w[j]), w_out=gdn_w_out[j].astype(BF16))
            x, s_p, s_s, c_p, c_s = _gdn_layer(x, T_p, B, S, Bd, Ld, state_gdn_s[j], state_gdn_conv[j], p,
                                               g1, b1, f"l{i}")
            res["gdn_s_p"].append(s_p)
            res["gdn_s_s"].append(s_s)
            res["gdn_c_p"].append(c_p)
            res["gdn_c_s"].append(c_s)
        else:
            W = SWA_HEADS * SWA_HD
            w_in = swa_w_in[j].astype(BF16)
            p = dict(w_in=[w_in[:, gi * 3 * W:(gi + 1) * 3 * W] for gi in range(SWA_GROUPS)],
                     w_out=swa_w_out[j].astype(BF16))
            x, new_p, new_s = _swa_layer(x, T_p, B, S, Bd, Ld, tuple(c[j] for c in caches), p,
                                         rope_cos, rope_ss, g1, b1, f"l{i}")
            for gi in range(SWA_GROUPS):
                swa_p[gi].append(new_p[gi])
                swa_s[gi].append(new_s[gi])
        x = _moe_ln(x, rw_pad, rb_pad, moe_w_gate, moe_w_up, moe_w_down, i,
                    row(ln2_g[i]), row(ln2_b[i]), f"l{i}")

    y_prompt = x[:T_p].reshape(B, S, D)
    y_sample = x[T_p:].reshape(Bd, SAMPLE_PAD, D)[:, :Ld]
    st = lambda k: jnp.stack(res[k])
    return (y_prompt, y_sample, st("lru_h_p"), st("lru_h_s"), st("lru_c_p"), st("lru_c_s"),
            st("gdn_s_p"), st("gdn_s_s"), st("gdn_c_p"), st("gdn_c_s"),
            jnp.stack(swa_p[0]), jnp.stack(swa_s[0]), jnp.stack(swa_p[1]), jnp.stack(swa_s[1]),
            jnp.stack(swa_p[2]), jnp.stack(swa_s[2]))
```

```python
import functools
import math

import jax
import jax.numpy as jnp
from jax import lax
from jax.experimental import pallas as pl
from jax.experimental.pallas import tpu as pltpu

F32 = jnp.float32
BF16 = jnp.bfloat16
I32 = jnp.int32

DEPTH = 4
LAYER_KIND = (0, 1, 2, 0)
KIND_INDEX = (0, 0, 0, 1)
DN_ALPHA = (2 * DEPTH) ** 0.25
LN_EPS = 1e-5
NORM_EPS = 1e-6
CONV_WIDTH = 4
LRU_BLOCKS = 4
LRU_C = 8.0
GDN_HK = 8
GDN_HV = 16
GDN_DK = 128
GDN_DV = 128
GDN_CHUNK = 64
SWA_WINDOWS = (128, 512, 2048)
SWA_DILATIONS = (1, 4, 16)
SWA_GROUPS = 3
SWA_HEADS = 16
SWA_HD = 64
SWA_BLOCK = 128
ROPE_THETA = 10000.0
PAST_LEN = 8192
N_EXPERTS = 32
N_EXPERT_GROUPS = 4
EXPERTS_PER_GROUP = 8

LANES = 128
SUBLANES = 8
SAMPLE_PAD = 8
SWA_TILE = 2048
MOE_ROWS = 512
TOK_BLK = 256
VMEM_LIMIT = 56 * 1024 * 1024


def _cparams(sem):
    return pltpu.CompilerParams(dimension_semantics=sem, vmem_limit_bytes=VMEM_LIMIT)


def _pick(n, cands):
    for c in cands:
        if n % c == 0:
            return c
    raise ValueError(f"no tile for {n}")


def _bdot(a, b):
    return jnp.dot(a.astype(BF16), b.astype(BF16), preferred_element_type=F32)


def _nt(a, b):
    return lax.dot_general(a.astype(BF16), b.astype(BF16), (((1,), (1,)), ((), ())),
                           preferred_element_type=F32)


def _tn(a, b):
    return lax.dot_general(a.astype(BF16), b.astype(BF16), (((0,), (0,)), ((), ())),
                           preferred_element_type=F32)


def _split(a):
    hi = a.astype(BF16)
    return hi, (a - hi.astype(F32)).astype(BF16)


def _dot3(a, b):
    (ah, al), (bh, bl) = a, b
    return (jnp.dot(ah, bh, preferred_element_type=F32)
            + (jnp.dot(ah, bl, preferred_element_type=F32) + jnp.dot(al, bh, preferred_element_type=F32)))


def _sigmoid(x):
    return 1.0 / (1.0 + jnp.exp(-x))


def _softplus(x):
    return jnp.maximum(x, 0.0) + jnp.log(1.0 + jnp.exp(-jnp.abs(x)))


def _layer_norm(x, g, b):
    mu = jnp.mean(x, -1, keepdims=True)
    xc = x - mu
    var = jnp.mean(xc * xc, -1, keepdims=True)
    return xc * lax.rsqrt(var + LN_EPS) * g + b


def _row_scan_sum(x, n):
    rows = lax.broadcasted_iota(I32, x.shape, 0)
    s = 1
    while s < n:
        x = x + jnp.where(rows >= s, pltpu.roll(x, s, 0), 0.0)
        s *= 2
    return x


def _proj_kernel(x_ref, w_ref, b_ref, o_ref):
    o_ref[...] = jnp.dot(x_ref[...].astype(BF16), w_ref[...], preferred_element_type=F32) + b_ref[...]


def _proj(x, w_bf, b, name):
    M, K = x.shape
    N = w_bf.shape[1]
    tm = _pick(M, (1280, 1024, 512, 256, 128, 64, 8))
    tn = _pick(N, (1024, 896, 768, 512, 384, 256, 128))
    return pl.pallas_call(
        _proj_kernel, grid=(M // tm, N // tn),
        in_specs=[pl.BlockSpec((tm, K), lambda i, j: (i, 0)),
                  pl.BlockSpec((K, tn), lambda i, j: (0, j)),
                  pl.BlockSpec((1, tn), lambda i, j: (0, j))],
        out_specs=pl.BlockSpec((tm, tn), lambda i, j: (i, j)),
        out_shape=jax.ShapeDtypeStruct((M, N), F32),
        compiler_params=_cparams(("parallel", "parallel")), name=name)(x, w_bf, b)


def _outproj_ln_kernel(h_ref, w_ref, r_ref, g_ref, b_ref, o_ref):
    y = jnp.dot(h_ref[...].astype(BF16), w_ref[...], preferred_element_type=F32)
    o_ref[...] = _layer_norm(DN_ALPHA * r_ref[...] + y, g_ref[...], b_ref[...])


def _outproj_ln(h, w_bf, res, g, b, name):
    M, K = h.shape
    N = w_bf.shape[1]
    tm = _pick(M, (1280, 1024, 512, 256, 128, 64, 8))
    return pl.pallas_call(
        _outproj_ln_kernel, grid=(M // tm,),
        in_specs=[pl.BlockSpec((tm, K), lambda i: (i, 0)),
                  pl.BlockSpec((K, N), lambda i: (0, 0)),
                  pl.BlockSpec((tm, N), lambda i: (i, 0)),
                  pl.BlockSpec((1, N), lambda i: (0, 0)),
                  pl.BlockSpec((1, N), lambda i: (0, 0))],
        out_specs=pl.BlockSpec((tm, N), lambda i: (i, 0)),
        out_shape=jax.ShapeDtypeStruct((M, N), F32),
        compiler_params=_cparams(("parallel",)), name=name)(h, w_bf, res, g, b)


def _lru_kernel(gate_ref, u_ref, h0_ref, cb0_ref, cw_ref, cbias_ref, wai_ref, ba_ref, bi_ref,
                lam_ref, o_ref, hl_ref, ext, hcar, *, Lc, L_valid, R):
    c = pl.program_id(1)

    @pl.when(c == 0)
    def _():
        ext[0:SUBLANES, :] = cb0_ref[0]
        hcar[...] = h0_ref[0][0:1, :]

    @pl.when(c > 0)
    def _():
        ext[0:SUBLANES, :] = ext[Lc:Lc + SUBLANES, :]

    ext[SUBLANES:SUBLANES + Lc, :] = u_ref[...]
    u = cbias_ref[...] + cw_ref[3:4, :] * ext[8:8 + Lc, :]
    for k in range(CONV_WIDTH - 1):
        u = u + cw_ref[k:k + 1, :] * ext[5 + k:5 + k + Lc, :]

    bw = R // LRU_BLOCKS
    za, zi = [], []
    for n in range(LRU_BLOCKS):
        z = jnp.dot(u[:, n * bw:(n + 1) * bw].astype(BF16), wai_ref[n], preferred_element_type=F32)
        za.append(z[:, :bw])
        zi.append(z[:, bw:])
    r = _sigmoid(jnp.concatenate(za, axis=1) + ba_ref[...])
    ig = _sigmoid(jnp.concatenate(zi, axis=1) + bi_ref[...])
    log_a = (-LRU_C) * r * _softplus(-lam_ref[...])
    a = jnp.exp(log_a)
    th = jnp.tanh(log_a)
    xin = jnp.sqrt(-2.0 * th / (1.0 - th)) * ig * u

    rows = lax.broadcasted_iota(I32, (Lc, R), 0)
    if L_valid is not None:
        valid = (rows + c * Lc) < L_valid
        a = jnp.where(valid, a, 1.0)
        xin = jnp.where(valid, xin, 0.0)

    in_tile = rows % SUBLANES
    s = 1
    while s < SUBLANES:
        a_sh = jnp.where(in_tile >= s, pltpu.roll(a, s, 0), 1.0)
        b_sh = jnp.where(in_tile >= s, pltpu.roll(xin, s, 0), 0.0)
        xin = a * b_sh + xin
        a = a * a_sh
        s *= 2
    gel = jax.nn.gelu(gate_ref[...], approximate=True)
    h_prev = hcar[...]
    for t in range(Lc // SUBLANES):
        rs = slice(t * SUBLANES, (t + 1) * SUBLANES)
        h_t = a[rs, :] * h_prev + xin[rs, :]
        o_ref[rs, :] = h_t * gel[rs, :]
        h_prev = h_t[SUBLANES - 1:SUBLANES, :]
    hcar[...] = h_prev
    hl_ref[0] = jnp.broadcast_to(h_prev, (SUBLANES, R))


def _lru_seq(proj, prev_out, h0, cb0, p, *, row_off, B, Lp, Lc, L_valid, name):
    T_all, R2 = proj.shape
    R = R2 // 2
    nC = Lp // Lc
    off = row_off // Lc
    kern = functools.partial(_lru_kernel, Lc=Lc, L_valid=L_valid, R=R)
    row_map = lambda b, c: (off + b * nC + c, 0)
    const2 = lambda b, c: (0, 0)
    in_specs = [pl.BlockSpec((Lc, R), row_map),
                pl.BlockSpec((Lc, R), lambda b, c: (off + b * nC + c, 1)),
                pl.BlockSpec((1, SUBLANES, R), lambda b, c: (b, 0, 0)),
                pl.BlockSpec((1, SUBLANES, R), lambda b, c: (b, 0, 0)),
                pl.BlockSpec((SUBLANES, R), const2),
                pl.BlockSpec((1, R), const2),
                pl.BlockSpec((LRU_BLOCKS, R // LRU_BLOCKS, 2 * R // LRU_BLOCKS), lambda b, c: (0, 0, 0)),
                pl.BlockSpec((1, R), const2), pl.BlockSpec((1, R), const2), pl.BlockSpec((1, R), const2)]
    args = [proj, proj, h0, cb0, p["cw"], p["cb"], p["wai"], p["ba"], p["bi"], p["lam"]]
    aliases = {}
    if prev_out is not None:
        in_specs.append(pl.BlockSpec(memory_space=pl.ANY))
        args.append(prev_out)
        aliases = {len(args) - 1: 0}
        kern = _drop_extra_input(kern, n_in=10, n_extra=1)
    out, hl = pl.pallas_call(
        kern, grid=(B, nC), in_specs=in_specs,
        out_specs=[pl.BlockSpec((Lc, R), row_map), pl.BlockSpec((1, SUBLANES, R), lambda b, c: (b, 0, 0))],
        out_shape=[jax.ShapeDtypeStruct((T_all, R), F32), jax.ShapeDtypeStruct((B, SUBLANES, R), F32)],
        scratch_shapes=[pltpu.VMEM((Lc + SUBLANES, R), F32), pltpu.VMEM((1, R), F32)],
        input_output_aliases=aliases,
        compiler_params=_cparams(("parallel", "arbitrary")), name=name)(*args)
    return out, hl[:, 0, :]


def _drop_extra_input(kern, n_in, n_extra):
    def wrapped(*refs):
        return kern(*refs[:n_in], *refs[n_in + n_extra:])
    return wrapped


def _gdn_kernel(qkv_ref, z_ref, ba_ref, s0_ref, cb0_ref, cw_ref, alog_ref, dtb_ref, nw_ref,
                o_ref, sfin_ref, ext, S, *, C, L_valid, nC):
    c = pl.program_id(1)
    QK = GDN_HK * GDN_DK

    @pl.when(c == 0)
    def _():
        ext[0:SUBLANES, :] = cb0_ref[0]
        S[...] = s0_ref[0]

    @pl.when(c > 0)
    def _():
        ext[0:SUBLANES, :] = ext[C:C + SUBLANES, :]

    ext[SUBLANES:SUBLANES + C, :] = qkv_ref[...]
    xc = cw_ref[3:4, :] * ext[8:8 + C, :]
    for k in range(CONV_WIDTH - 1):
        xc = xc + cw_ref[k:k + 1, :] * ext[5 + k:5 + k + C, :]
    act = xc * _sigmoid(xc)

    ba = ba_ref[...]
    beta = _sigmoid(ba)
    g = pltpu.roll(-jnp.exp(alog_ref[...]) * _softplus(ba + dtb_ref[...]), LANES - GDN_HV, 1)
    if L_valid is not None:
        rows = lax.broadcasted_iota(I32, (C, LANES), 0)
        valid = (rows + c * C) < L_valid
        g = jnp.where(valid, g, 0.0)
        beta = jnp.where(valid, beta, 0.0)
    gc = _row_scan_sum(g, C)
    gct = jnp.transpose(gc)
    e_gc = jnp.exp(gc)
    g_last = gc[C - 1:C, :]
    e_rest = jnp.exp(g_last - gc)
    e_last = jnp.exp(g_last)

    ri = lax.broadcasted_iota(I32, (C, C), 0)
    ci = lax.broadcasted_iota(I32, (C, C), 1)
    tril = ri >= ci
    strict = ri > ci

    qn, kn = [], []
    for hk in range(GDN_HK):
        qh = act[:, hk * GDN_DK:(hk + 1) * GDN_DK]
        kh = act[:, QK + hk * GDN_DK:QK + (hk + 1) * GDN_DK]
        qn.append(qh * lax.rsqrt(jnp.sum(qh * qh, -1, keepdims=True) + NORM_EPS) * (GDN_DK ** -0.5))
        kn.append(kh * lax.rsqrt(jnp.sum(kh * kh, -1, keepdims=True) + NORM_EPS))

    rep = GDN_HV // GDN_HK
    heads = range(GDN_HV)
    qkk = [_nt(jnp.concatenate([qn[hk], kn[hk]], axis=0), kn[hk]) for hk in range(GDN_HK)]
    eye = jnp.where(ri == ci, 1.0, 0.0)
    decay, xs, ps, rhs = [], [], [], []
    for h in heads:
        kh = kn[h // rep]
        vh = act[:, 2 * QK + h * GDN_DV:2 * QK + (h + 1) * GDN_DV]
        bi = beta[:, h:h + 1]
        dec = jnp.exp(jnp.where(tril, gc[:, h:h + 1] - gct[h:h + 1, :], -jnp.inf))
        a_mat = jnp.where(strict, bi * qkk[h // rep][C:, :] * dec, 0.0)
        decay.append(dec)
        xs.append(eye - a_mat)
        ps.append(a_mat)
        rhs.append(jnp.concatenate([vh * bi, kh * (bi * e_gc[:, h:h + 1])], axis=1))
    p_sp = [_split(p) for p in ps]
    k = 2
    while k < C:
        ps = [_dot3(sp, sp) for sp in p_sp]
        p_sp = [_split(p) for p in ps]
        xs = [x + _dot3(_split(x), sp) for x, sp in zip(xs, p_sp)]
        k *= 2
    uw = [_bdot(x, r) for x, r in zip(xs, rhs)]
    ws = [_bdot(jnp.concatenate([uw[h][:, GDN_DV:], qn[h // rep] * e_gc[:, h:h + 1]], axis=0), S[h])
          for h in heads]
    v_new = [uw[h][:, :GDN_DV] - ws[h][:C, :] for h in heads]
    o_intra = [_bdot(qkk[h // rep][:C, :] * decay[h], v_new[h]) for h in heads]
    s_upd = [_tn(kn[h // rep] * e_rest[:, h:h + 1], v_new[h]) for h in heads]
    for h in heads:
        S[h] = S[h] * e_last[:, h:h + 1] + s_upd[h]
        o = ws[h][C:, :] + o_intra[h]
        o = o * lax.rsqrt(jnp.mean(o * o, -1, keepdims=True) + NORM_EPS) * nw_ref[...]
        zh = z_ref[:, h * GDN_DV:(h + 1) * GDN_DV]
        o_ref[:, h * GDN_DV:(h + 1) * GDN_DV] = o * (zh * _sigmoid(zh))

    @pl.when(c == nC - 1)
    def _():
        sfin_ref[0] = S[...]


def _gdn_seq(proj, prev_out, s0, cb0, p, *, row_off, B, Lp, C, L_valid, name):
    T_all = proj.shape[0]
    QK, VD = GDN_HK * GDN_DK, GDN_HV * GDN_DV
    CD = 2 * QK + VD
    nC = Lp // C
    off = row_off // C
    kern = functools.partial(_gdn_kernel, C=C, L_valid=L_valid, nC=nC)
    row_map = lambda b, c: (off + b * nC + c, 0)
    const2 = lambda b, c: (0, 0)
    zblk = CD // VD
    in_specs = [pl.BlockSpec((C, CD), row_map),
                pl.BlockSpec((C, VD), lambda b, c: (off + b * nC + c, zblk)),
                pl.BlockSpec((C, LANES), lambda b, c: (off + b * nC + c, (CD + VD) // LANES)),
                pl.BlockSpec((1, GDN_HV, GDN_DK, GDN_DV), lambda b, c: (b, 0, 0, 0)),
                pl.BlockSpec((1, SUBLANES, CD), lambda b, c: (b, 0, 0)),
                pl.BlockSpec((SUBLANES, CD), const2),
                pl.BlockSpec((1, LANES), const2), pl.BlockSpec((1, LANES), const2),
                pl.BlockSpec((1, GDN_DV), const2)]
    args = [proj, proj, proj, s0, cb0, p["cw"], p["alog"], p["dtb"], p["nw"]]
    aliases = {}
    if prev_out is not None:
        in_specs.append(pl.BlockSpec(memory_space=pl.ANY))
        args.append(prev_out)
        aliases = {len(args) - 1: 0}
        kern = _drop_extra_input(kern, n_in=9, n_extra=1)
    out, sfin = pl.pallas_call(
        kern, grid=(B, nC), in_specs=in_specs,
        out_specs=[pl.BlockSpec((C, VD), row_map),
                   pl.BlockSpec((1, GDN_HV, GDN_DK, GDN_DV), lambda b, c: (b, 0, 0, 0))],
        out_shape=[jax.ShapeDtypeStruct((T_all, VD), F32),
                   jax.ShapeDtypeStruct((B, GDN_HV, GDN_DK, GDN_DV), F32)],
        scratch_shapes=[pltpu.VMEM((C + SUBLANES, CD), F32), pltpu.VMEM((GDN_HV, GDN_DK, GDN_DV), F32)],
        input_output_aliases=aliases,
        compiler_params=_cparams(("parallel", "arbitrary")), name=name)(*args)
    return out, sfin


def _rope(x, cos, ss):
    W = x.shape[1]
    reps = W // LANES
    cfull = jnp.concatenate([cos] * reps, axis=1)
    sfull = jnp.concatenate([ss] * reps, axis=1)
    lane = lax.broadcasted_iota(I32, x.shape, 1)
    first = (lane % SWA_HD) < (SWA_HD // 2)
    partner = jnp.where(first, pltpu.roll(x, W - SWA_HD // 2, 1), pltpu.roll(x, SWA_HD // 2, 1))
    return x * cfull + partner * sfull


def _swa_proj_kernel(x_ref, w_ref, cos_ref, ss_ref, o_ref, acc_ref, *, d, tm, n_rot):
    j = pl.program_id(1)
    acc = jnp.dot(x_ref[...].astype(BF16), w_ref[...], preferred_element_type=F32)

    def emit(val):
        if d == 1:
            o_ref[...] = val
            return
        n = tm // d
        for cb in range(val.shape[1] // LANES):
            acc_ref[cb] = val[:, cb * LANES:(cb + 1) * LANES]
        for cb in range(val.shape[1] // LANES):
            for r in range(d):
                o_ref[r * n:(r + 1) * n, cb * LANES:(cb + 1) * LANES] = acc_ref[cb, pl.ds(r, n, stride=d), :]

    @pl.when(j < n_rot)
    def _():
        emit(_rope(acc, cos_ref[...], ss_ref[...]))

    @pl.when(j >= n_rot)
    def _():
        emit(acc)


def _swa_proj(x, w_bf, cos, ss, *, d, tm, row_blk_off, n_row_blks, name):
    K = x.shape[1]
    N = w_bf.shape[1]
    tn = 512
    n_rot = (2 * N // 3) // tn
    kern = functools.partial(_swa_proj_kernel, d=d, tm=tm, n_rot=n_rot)
    return pl.pallas_call(
        kern, grid=(n_row_blks, N // tn),
        in_specs=[pl.BlockSpec((tm, K), lambda i, j: (i + row_blk_off, 0)),
                  pl.BlockSpec((K, tn), lambda i, j: (0, j)),
                  pl.BlockSpec((tm, LANES), lambda i, j: (i + row_blk_off, 0)),
                  pl.BlockSpec((tm, LANES), lambda i, j: (i + row_blk_off, 0))],
        out_specs=pl.BlockSpec((tm, tn), lambda i, j: (i, j)),
        out_shape=jax.ShapeDtypeStruct((n_row_blks * tm, N), F32),
        scratch_shapes=[pltpu.VMEM((tn // LANES, tm, LANES), F32)],
        compiler_params=_cparams(("parallel", "parallel")), name=name)(x, w_bf, cos, ss)


def _swa_attn_kernel(q_ref, kc_ref, kp_ref, vc_ref, vp_ref, o_ref, l_ref, *, nbt):
    t = pl.program_id(1)
    j = pl.program_id(2)
    has_prev = jnp.logical_or(t > 0, (j % nbt) > 0)
    QB = SWA_BLOCK
    qi = lax.broadcasted_iota(I32, (QB, QB), 0)
    ki = lax.broadcasted_iota(I32, (QB, QB), 1)
    mask_c = ki <= qi
    mask_p = jnp.logical_and(ki >= qi, has_prev)
    sls = [slice(h * SWA_HD, (h + 1) * SWA_HD) for h in range(SWA_HEADS)]
    qs = [(q_ref[:, sl] * (SWA_HD ** -0.5)).astype(BF16) for sl in sls]
    scs = [_nt(q, kc_ref[:, sl]) for q, sl in zip(qs, sls)]
    sps = [_nt(q, kp_ref[:, sl]) for q, sl in zip(qs, sls)]
    pcs, pps, ms, ls = [], [], [], []
    for sc, sp in zip(scs, sps):
        sc = jnp.where(mask_c, sc, -jnp.inf)
        sp = jnp.where(mask_p, sp, -jnp.inf)
        m = jnp.maximum(jnp.max(sc, -1, keepdims=True), jnp.max(sp, -1, keepdims=True))
        pc = jnp.exp(sc - m)
        pp = jnp.exp(sp - m)
        l = jnp.sum(pc, -1, keepdims=True) + jnp.sum(pp, -1, keepdims=True)
        inv_l = 1.0 / l
        ls.append(l)
        ms.append(m)
        pcs.append((pc * inv_l).astype(BF16))
        pps.append((pp * inv_l).astype(BF16))
    ocs = [_bdot(pc, vc_ref[:, sl]) for pc, sl in zip(pcs, sls)]
    ops = [_bdot(pp, vp_ref[:, sl]) for pp, sl in zip(pps, sls)]
    for h, sl in enumerate(sls):
        o_ref[:, sl] = ocs[h] + ops[h]
        l_ref[:, sl] = jnp.broadcast_to(ms[h] + jnp.log(ls[h]), (QB, SWA_HD))


def _swa_attn(pg, *, B, S, d, name):
    W = SWA_HEADS * SWA_HD
    QB = SWA_BLOCK
    bpt = SWA_TILE // QB
    nbt = bpt // d
    nT = S // SWA_TILE
    nblk = S // QB

    def cur(col):
        return lambda b, t, j: (b * nblk + t * bpt + j, col)

    def prev(col):
        def f(b, t, j):
            first = (j % nbt) == 0
            pj = jnp.where(first, j + nbt - 1, j - 1)
            pt = jnp.where(first, jnp.maximum(t - 1, 0), t)
            return (b * nblk + pt * bpt + pj, col)
        return f

    blk = (QB, W)
    o, l = pl.pallas_call(
        functools.partial(_swa_attn_kernel, nbt=nbt), grid=(B, nT, bpt),
        in_specs=[pl.BlockSpec(blk, cur(0)), pl.BlockSpec(blk, cur(1)), pl.BlockSpec(blk, prev(1)),
                  pl.BlockSpec(blk, cur(2)), pl.BlockSpec(blk, prev(2))],
        out_specs=[pl.BlockSpec(blk, cur(0)), pl.BlockSpec(blk, cur(0))],
        out_shape=[jax.ShapeDtypeStruct((B * S, W), F32), jax.ShapeDtypeStruct((B * S, W), F32)],
        compiler_params=_cparams(("parallel", "parallel", "arbitrary")), name=name)(pg, pg, pg, pg, pg)
    return o, l


def _merge3(o0, l0, o1, l1, o2, l2):
    m = jnp.maximum(jnp.maximum(l0, l1), l2)
    w0 = jnp.exp(l0 - m)
    w1 = jnp.exp(l1 - m)
    w2 = jnp.exp(l2 - m)
    return (w0 * o0 + w1 * o1 + w2 * o2) / (w0 + w1 + w2)


def _swa_merge_kernel(o0, l0, o1, l1, o2, l2, out_ref, s1o, s1l, s2o, s2l, *, tm):
    for (src_o, src_l, dst_o, dst_l, d) in ((o1, l1, s1o, s1l, SWA_DILATIONS[1]),
                                            (o2, l2, s2o, s2l, SWA_DILATIONS[2])):
        n = tm // d
        for cb in range(dst_o.shape[0]):
            cs = slice(cb * LANES, (cb + 1) * LANES)
            for r in range(d):
                dst_o[cb, pl.ds(r, n, stride=d), :] = src_o[r * n:(r + 1) * n, cs]
                dst_l[cb, pl.ds(r, n, stride=d), :] = src_l[r * n:(r + 1) * n, cs]
    for cb in range(s1o.shape[0]):
        cs = slice(cb * LANES, (cb + 1) * LANES)
        out_ref[:, cs] = _merge3(o0[:, cs], l0[:, cs], s1o[cb], s1l[cb], s2o[cb], s2l[cb])


def _swa_merge(outs, lses, *, rows, total_rows, name):
    W = outs[0].shape[1]
    tm, tn = SWA_TILE, 256
    spec = pl.BlockSpec((tm, tn), lambda i, j: (i, j))
    args = [outs[0], lses[0], outs[1], lses[1], outs[2], lses[2]]
    return pl.pallas_call(
        functools.partial(_swa_merge_kernel, tm=tm), grid=(rows // tm, W // tn),
        in_specs=[spec] * 6, out_specs=spec,
        out_shape=jax.ShapeDtypeStruct((total_rows, W), F32),
        scratch_shapes=[pltpu.VMEM((tn // LANES, tm, LANES), F32)] * 4,
        compiler_params=_cparams(("parallel", "parallel")), name=name)(*args)


def _swa_sample_kernel(x_ref, c_ref, o_ref, l_ref, *, d, window, Wb):
    L = SAMPLE_PAD
    W = SWA_HEADS * SWA_HD
    ccol = lax.broadcasted_iota(I32, (L, Wb), 1)
    lrow = lax.broadcasted_iota(I32, (L, Wb), 0)
    dist = Wb + lrow - ccol
    mask_c = jnp.logical_and(dist % d == 0, dist <= window)
    ln = lax.broadcasted_iota(I32, (L, L), 0)
    lk = lax.broadcasted_iota(I32, (L, L), 1)
    dn = ln - lk
    mask_n = jnp.logical_and(jnp.logical_and(dn >= 0, dn % d == 0), dn <= window)
    heads = range(SWA_HEADS)
    sls = [slice(h * SWA_HD, (h + 1) * SWA_HD) for h in heads]
    qs = [(x_ref[:, sl] * (SWA_HD ** -0.5)).astype(BF16) for sl in sls]
    scs = [_bdot(qs[h], c_ref[0, h]) for h in heads]
    sns = [_nt(qs[h], x_ref[:, W + h * SWA_HD:W + (h + 1) * SWA_HD]) for h in heads]
    pcs, pns, ms, ls = [], [], [], []
    for sc, sn in zip(scs, sns):
        sc = jnp.where(mask_c, sc, -jnp.inf)
        sn = jnp.where(mask_n, sn, -jnp.inf)
        m = jnp.maximum(jnp.max(sc, -1, keepdims=True), jnp.max(sn, -1, keepdims=True))
        pc = jnp.exp(sc - m)
        pn = jnp.exp(sn - m)
        l = jnp.sum(pc, -1, keepdims=True) + jnp.sum(pn, -1, keepdims=True)
        inv_l = 1.0 / l
        ls.append(l)
        ms.append(m)
        pcs.append((pc * inv_l).astype(BF16))
        pns.append((pn * inv_l).astype(BF16))
    ocs = [_nt(pcs[h], c_ref[1, h]) for h in heads]
    ons = [_bdot(pns[h], x_ref[:, 2 * W + h * SWA_HD:2 * W + (h + 1) * SWA_HD]) for h in heads]
    for h, sl in enumerate(sls):
        o_ref[:, sl] = ocs[h] + ons[h]
        l_ref[:, sl] = jnp.broadcast_to(ms[h] + jnp.log(ls[h]), (L, SWA_HD))


def _swa_sample_attn(ps, cache, *, d, window, name):
    Bd, Wb = cache.shape[0], cache.shape[1]
    H, hd = SWA_HEADS, SWA_HD
    W = H * hd
    L = SAMPLE_PAD
    assert Wb % d == 0
    ct = jnp.transpose(cache, (0, 2, 3, 4, 1))
    kern = functools.partial(_swa_sample_kernel, d=d, window=window, Wb=Wb)
    oblk = pl.BlockSpec((L, W), lambda b: (b, 0))
    o, l = pl.pallas_call(
        kern, grid=(Bd,),
        in_specs=[pl.BlockSpec((L, 3 * W), lambda b: (b, 0)),
                  pl.BlockSpec((None, 2, H, hd, Wb), lambda b: (b, 0, 0, 0, 0))],
        out_specs=[oblk, oblk],
        out_shape=[jax.ShapeDtypeStruct((Bd * L, W), F32)] * 2,
        compiler_params=_cparams(("parallel",)), name=name)(ps, ct)
    return o, l


def _merge_plain_kernel(o0, l0, o1, l1, o2, l2, out_ref):
    out_ref[...] = _merge3(o0[...], l0[...], o1[...], l1[...], o2[...], l2[...])


def _swa_merge_sample(outs, lses, prev_out, *, row_off, name):
    rows, W = outs[0].shape
    T_all = prev_out.shape[0]
    off = row_off // rows
    spec = pl.BlockSpec((rows, W), lambda i: (0, 0))
    args = [outs[0], lses[0], outs[1], lses[1], outs[2], lses[2], prev_out]
    kern = _drop_extra_input(_merge_plain_kernel, n_in=6, n_extra=1)
    return pl.pallas_call(
        kern, grid=(1,),
        in_specs=[spec] * 6 + [pl.BlockSpec(memory_space=pl.ANY)],
        out_specs=pl.BlockSpec((rows, W), lambda i: (off, 0)),
        out_shape=jax.ShapeDtypeStruct((T_all, W), F32),
        input_output_aliases={6: 0},
        compiler_params=_cparams(("arbitrary",)), name=name)(*args)


def _router_kernel(x_ref, w_ref, b_ref, info_ref, cnt_ref, car, *, tm):
    i = pl.program_id(0)

    @pl.when(i == 0)
    def _():
        car[...] = jnp.zeros_like(car)

    logits = jnp.dot(x_ref[...], w_ref[...], precision=lax.Precision.HIGHEST,
                     preferred_element_type=F32) + b_ref[...]
    m = jnp.max(logits, -1, keepdims=True)
    e = jnp.exp(logits - m)
    probs = e / jnp.sum(e, -1, keepdims=True)
    lane = lax.broadcasted_iota(I32, (tm, LANES), 1).astype(F32)
    BIG = 4.0 * LANES

    best = None
    for gi in range(N_EXPERT_GROUPS):
        lo = float(gi * EXPERTS_PER_GROUP)
        masked = jnp.where(jnp.logical_and(lane >= lo, lane < lo + EXPERTS_PER_GROUP), probs, -1.0)
        m1 = jnp.max(masked, -1, keepdims=True)
        i1 = jnp.min(jnp.where(masked == m1, lane, BIG), -1, keepdims=True)
        masked2 = jnp.where(lane == i1, -1.0, masked)
        m2 = jnp.max(masked2, -1, keepdims=True)
        i2 = jnp.min(jnp.where(masked2 == m2, lane, BIG), -1, keepdims=True)
        score = m1 + m2
        if best is None:
            best = (score, m1, i1, m2, i2)
        else:
            take = score > best[0]
            best = tuple(jnp.where(take, n, o) for n, o in zip((score, m1, i1, m2, i2), best))
    _, m1, i1, m2, i2 = best
    wsum = m1 + m2
    w1 = m1 / wsum
    w2 = m2 / wsum

    onehot = jnp.where(jnp.logical_or(lane == i1, lane == i2), 1.0, 0.0)
    ri = lax.broadcasted_iota(I32, (tm, tm), 0)
    ci = lax.broadcasted_iota(I32, (tm, tm), 1)
    lower = jnp.where(ri > ci, 1.0, 0.0)
    before = _bdot(lower, onehot) + car[...]
    r1 = jnp.sum(jnp.where(lane == i1, before, 0.0), -1, keepdims=True)
    r2 = jnp.sum(jnp.where(lane == i2, before, 0.0), -1, keepdims=True)
    car[...] = car[...] + jnp.sum(onehot, 0, keepdims=True)
    cnt_ref[...] = jnp.broadcast_to(car[...], cnt_ref.shape)

    info = jnp.zeros((tm, LANES), F32)
    for k, val in enumerate((i1, i2, w1, w2, r1, r2)):
        info = jnp.where(lane == float(k), val, info)
    info_ref[...] = info


def _router(x, rw_pad, rb_pad, name):
    T, D = x.shape
    tm = _pick(T, (256, 128, 64, 8))
    info, cnt = pl.pallas_call(
        functools.partial(_router_kernel, tm=tm), grid=(T // tm,),
        in_specs=[pl.BlockSpec((tm, D), lambda i: (i, 0)),
                  pl.BlockSpec((D, LANES), lambda i: (0, 0)),
                  pl.BlockSpec((1, LANES), lambda i: (0, 0))],
        out_specs=[pl.BlockSpec((tm, LANES), lambda i: (i, 0)),
                   pl.BlockSpec((SUBLANES, LANES), lambda i: (0, 0))],
        out_shape=[jax.ShapeDtypeStruct((T, LANES), F32), jax.ShapeDtypeStruct((SUBLANES, LANES), F32)],
        scratch_shapes=[pltpu.VMEM((1, LANES), F32)],
        compiler_params=_cparams(("arbitrary",)), name=name)(x, rw_pad, rb_pad)
    return info, cnt


def _row_copy(src, dst, s, t, sem):
    return pltpu.make_async_copy(src.at[pl.ds(s, 1), :], dst.at[pl.ds(t, 1), :], sem)


DISPATCH_SLOTS = 3
ISSUE_UNROLL = 8


def _dispatch_kernel(dest_ref, x_hbm, xs_out, buf, lsem, ssem, *, nblk):
    i = pl.program_id(0)

    def load(blk):
        sl = blk % DISPATCH_SLOTS
        return pltpu.make_async_copy(x_hbm.at[pl.ds(blk * TOK_BLK, TOK_BLK), :], buf.at[sl], lsem.at[sl])

    def drain(blk):
        sl = blk % DISPATCH_SLOTS
        for a in range(2):
            pltpu.make_async_copy(buf.at[sl], xs_out.at[pl.ds(0, TOK_BLK), :], ssem.at[sl]).wait()

    @pl.when(i == 0)
    def _():
        load(0).start()
        if nblk > 1:
            load(1).start()

    slot = i % DISPATCH_SLOTS
    load(i).wait()

    def issue(k, carry):
        for a in range(2):
            _row_copy(buf.at[slot], xs_out, k, dest_ref[0, 0, 2 * k + a], ssem.at[slot]).start(priority=a)
        return carry

    lax.fori_loop(0, TOK_BLK, issue, 0, unroll=ISSUE_UNROLL)

    @pl.when(i > 0)
    def _():
        drain(i - 1)

    @pl.when(i + 2 < nblk)
    def _():
        load(i + 2).start()

    @pl.when(i == nblk - 1)
    def _():
        drain(i)


def _dispatch(x, dest, n_rows, name):
    T, D = x.shape
    nblk = T // TOK_BLK
    dest3 = dest.reshape(nblk, 1, 2 * TOK_BLK)
    return pl.pallas_call(
        functools.partial(_dispatch_kernel, nblk=nblk), grid=(nblk,),
        in_specs=[pl.BlockSpec((1, 1, 2 * TOK_BLK), lambda i: (i, 0, 0), memory_space=pltpu.SMEM),
                  pl.BlockSpec(memory_space=pl.ANY)],
        out_specs=pl.BlockSpec(memory_space=pl.ANY),
        out_shape=jax.ShapeDtypeStruct((n_rows, D), F32),
        scratch_shapes=[pltpu.VMEM((DISPATCH_SLOTS, TOK_BLK, D), F32),
                        pltpu.SemaphoreType.DMA((DISPATCH_SLOTS,)),
                        pltpu.SemaphoreType.DMA((DISPATCH_SLOTS,))],
        compiler_params=_cparams(("arbitrary",)), name=name)(dest3, x)


def _expert_kernel(be_ref, nu_ref, nv_ref, x_ref, wg_ref, wu_ref, wd_ref, o_ref, wg_bf, wu_bf, wd_bf):
    i = pl.program_id(0)
    active = i < nu_ref[0]

    @pl.when(jnp.logical_and(active, jnp.logical_or(i == 0, be_ref[i] != be_ref[jnp.maximum(i - 1, 0)])))
    def _():
        wg_bf[...] = wg_ref[0].astype(BF16)
        wu_bf[...] = wu_ref[0].astype(BF16)
        wd_bf[...] = wd_ref[0].astype(BF16)

    @pl.when(active)
    def _():
        rows = lax.broadcasted_iota(I32, x_ref.shape, 0)
        x = jnp.where(rows < nv_ref[i], x_ref[...], 0.0).astype(BF16)
        g = jnp.dot(x, wg_bf[...], preferred_element_type=F32)
        u = jnp.dot(x, wu_bf[...], preferred_element_type=F32)
        hid = (g * _sigmoid(g)) * u
        o_ref[...] = jnp.dot(hid.astype(BF16), wd_bf[...], preferred_element_type=F32)

    @pl.when(i >= nu_ref[0])
    def _():
        o_ref[...] = jnp.zeros_like(o_ref)


def _experts(xs, blk_e, n_used, blk_valid, w_gate, w_up, w_down, layer, name):
    n_rows, D = xs.shape
    F = w_gate.shape[3]
    nb = n_rows // MOE_ROWS
    grid_spec = pltpu.PrefetchScalarGridSpec(
        num_scalar_prefetch=3, grid=(nb,),
        in_specs=[pl.BlockSpec((MOE_ROWS, D), lambda i, be, nu, nv: (i, 0)),
                  pl.BlockSpec((None, 1, D, F), lambda i, be, nu, nv: (layer, be[i], 0, 0)),
                  pl.BlockSpec((None, 1, D, F), lambda i, be, nu, nv: (layer, be[i], 0, 0)),
                  pl.BlockSpec((None, 1, F, D), lambda i, be, nu, nv: (layer, be[i], 0, 0))],
        out_specs=pl.BlockSpec((MOE_ROWS, D), lambda i, be, nu, nv: (i, 0)),
        scratch_shapes=[pltpu.VMEM((D, F), BF16), pltpu.VMEM((D, F), BF16), pltpu.VMEM((F, D), BF16)])
    return pl.pallas_call(
        _expert_kernel, grid_spec=grid_spec,
        out_shape=jax.ShapeDtypeStruct((n_rows, D), F32),
        compiler_params=_cparams(("arbitrary",)), name=name)(blk_e, n_used, blk_valid, xs, w_gate, w_up, w_down)


def _combine_kernel(dcur_ref, dnext_ref, x_ref, info_ref, g_ref, b_ref, ys_hbm, o_ref, buf, sems, *, nblk):
    i = pl.program_id(0)
    slot = i % 2

    def issue(dref, sl):
        def body(k, carry):
            for a in range(2):
                _row_copy(ys_hbm, buf.at[sl, a], dref[0, 0, 2 * k + a], k, sems.at[sl]).start(priority=a)
            return carry
        lax.fori_loop(0, TOK_BLK, body, 0, unroll=ISSUE_UNROLL)

    @pl.when(i == 0)
    def _():
        issue(dcur_ref, 0)

    @pl.when(i + 1 < nblk)
    def _():
        issue(dnext_ref, 1 - slot)

    for a in range(2):
        pltpu.make_async_copy(ys_hbm.at[pl.ds(0, TOK_BLK), :], buf.at[slot, a], sems.at[slot]).wait()
    info = info_ref[...]
    y = info[:, 2:3] * buf[slot, 0] + info[:, 3:4] * buf[slot, 1]
    o_ref[...] = _layer_norm(DN_ALPHA * x_ref[...] + y, g_ref[...], b_ref[...])


def _combine(x, info, dest, ys, g, b, name):
    T, D = x.shape
    nblk = T // TOK_BLK
    dest3 = dest.reshape(nblk, 1, 2 * TOK_BLK)
    dspec = lambda f: pl.BlockSpec((1, 1, 2 * TOK_BLK), f, memory_space=pltpu.SMEM)
    return pl.pallas_call(
        functools.partial(_combine_kernel, nblk=nblk), grid=(nblk,),
        in_specs=[dspec(lambda i: (i, 0, 0)),
                  dspec(lambda i: (jnp.minimum(i + 1, nblk - 1), 0, 0)),
                  pl.BlockSpec((TOK_BLK, D), lambda i: (i, 0)),
                  pl.BlockSpec((TOK_BLK, LANES), lambda i: (i, 0)),
                  pl.BlockSpec((1, D), lambda i: (0, 0)), pl.BlockSpec((1, D), lambda i: (0, 0)),
                  pl.BlockSpec(memory_space=pl.ANY)],
        out_specs=pl.BlockSpec((TOK_BLK, D), lambda i: (i, 0)),
        out_shape=jax.ShapeDtypeStruct((T, D), F32),
        scratch_shapes=[pltpu.VMEM((2, 2, TOK_BLK, D), F32), pltpu.SemaphoreType.DMA((2,))],
        compiler_params=_cparams(("arbitrary",)), name=name)(dest3, dest3, x, info, g, b, ys)


def _moe_ln(x, rw_pad, rb_pad, w_gate, w_up, w_down, layer, g, b, tag):
    T, D = x.shape
    info, cnt = _router(x, rw_pad, rb_pad, f"router_{tag}")
    e = info[:, 0:2].astype(I32)
    rank = info[:, 4:6].astype(I32)
    counts = cnt[0, :N_EXPERTS].astype(I32)
    padded = (counts + MOE_ROWS - 1) // MOE_ROWS * MOE_ROWS
    pend = jnp.cumsum(padded)
    pstart = pend - padded
    dest = (pstart[e] + rank).reshape(-1)
    n_blocks = (2 * T + N_EXPERTS * (MOE_ROWS - 1) + MOE_ROWS - 1) // MOE_ROWS
    blk_start = jnp.arange(n_blocks, dtype=I32) * MOE_ROWS
    blk_e = jnp.minimum(jnp.sum((pend[None, :] <= blk_start[:, None]).astype(I32), axis=1), N_EXPERTS - 1)
    n_used = (pend[-1:] // MOE_ROWS).astype(I32)
    blk_valid = jnp.clip((pstart + counts)[blk_e] - blk_start, 0, MOE_ROWS).astype(I32)
    xs = _dispatch(x, dest, n_blocks * MOE_ROWS, f"dispatch_{tag}")
    ys = _experts(xs, blk_e, n_used, blk_valid, w_gate, w_up, w_down, layer, f"experts_{tag}")
    return _combine(x, info, dest, ys, g, b, f"combine_{tag}")


def _tail_rows(a2d, B, S, n, col0, ncol):
    return jnp.stack([a2d[b * S + S - n:b * S + S, col0:col0 + ncol] for b in range(B)])


def _pad_tail(buf):
    return jnp.pad(buf, ((0, 0), (SUBLANES - (CONV_WIDTH - 1), 0), (0, 0)))


def _lru_layer(x, T_p, B, S, Bd, Ld, state_h, state_conv, p, ln_g, ln_b, tag):
    R = p["lam"].shape[1]
    proj = _proj(x, p["w_in"], p["b_in"], f"lru_in_{tag}")
    zeros_h = jnp.zeros((B, SUBLANES, R), F32)
    hg, h_p = _lru_seq(proj, None, zeros_h, zeros_h, p, row_off=0, B=B, Lp=S,
                       Lc=_pick(S, (256, 128, 64, 8)), L_valid=None, name=f"lru_seq_p_{tag}")
    h0 = jnp.broadcast_to(state_h[:, None, :], (Bd, SUBLANES, R))
    hg, h_s = _lru_seq(proj, hg, h0, _pad_tail(state_conv), p, row_off=T_p, B=Bd, Lp=SAMPLE_PAD,
                       Lc=SAMPLE_PAD, L_valid=Ld, name=f"lru_seq_s_{tag}")
    x_new = _outproj_ln(hg, p["w_out"], x, ln_g, ln_b, f"lru_out_{tag}")
    u_s = proj[T_p:, R:].reshape(Bd, SAMPLE_PAD, R)
    conv_p = _tail_rows(proj, B, S, CONV_WIDTH - 1, R, R)
    conv_s = jnp.concatenate([state_conv, u_s[:, :Ld]], axis=1)[:, -(CONV_WIDTH - 1):]
    return x_new, h_p, h_s, conv_p, conv_s


def _gdn_layer(x, T_p, B, S, Bd, Ld, state_s, state_conv, p, ln_g, ln_b, tag):
    QK, VD = GDN_HK * GDN_DK, GDN_HV * GDN_DV
    CD = 2 * QK + VD
    proj = _proj(x, p["w_main"], p["b_main"], f"gdn_in_{tag}")
    o, s_p = _gdn_seq(proj, None, jnp.zeros((B, GDN_HV, GDN_DK, GDN_DV), F32),
                      jnp.zeros((B, SUBLANES, CD), F32), p, row_off=0, B=B, Lp=S,
                      C=math.gcd(S, GDN_CHUNK), L_valid=None, name=f"gdn_seq_p_{tag}")
    o, s_s = _gdn_seq(proj, o, state_s, _pad_tail(state_conv), p, row_off=T_p, B=Bd, Lp=SAMPLE_PAD,
                      C=SAMPLE_PAD, L_valid=Ld, name=f"gdn_seq_s_{tag}")
    x_new = _outproj_ln(o, p["w_out"], x, ln_g, ln_b, f"gdn_out_{tag}")
    q_s = proj[T_p:, :CD].reshape(Bd, SAMPLE_PAD, CD)
    conv_p = _tail_rows(proj, B, S, CONV_WIDTH - 1, 0, CD)
    conv_s = jnp.concatenate([state_conv, q_s[:, :Ld]], axis=1)[:, -(CONV_WIDTH - 1):]
    return x_new, s_p, s_s, conv_p, conv_s


def _unpermute_tail(pg, B, S, d, keep, col0, ncol):
    tiles = -(-keep // SWA_TILE)
    rows = tiles * SWA_TILE
    t = _tail_rows(pg, B, S, rows, col0, ncol)
    t = t.reshape(B, tiles, d, SWA_TILE // d, ncol).transpose(0, 1, 3, 2, 4).reshape(B, rows, ncol)
    return t[:, rows - keep:]


def _swa_layer(x, T_p, B, S, Bd, Ld, caches, p, rope_cos, rope_ss, ln_g, ln_b, tag):
    W = SWA_HEADS * SWA_HD
    T_all = x.shape[0]
    n_s = T_all - T_p
    outs, lses, outs_s, lses_s, new_p, new_s = [], [], [], [], [], []
    for gi in range(SWA_GROUPS):
        d, window = SWA_DILATIONS[gi], SWA_WINDOWS[gi]
        pg = _swa_proj(x, p["w_in"][gi], rope_cos, rope_ss, d=d, tm=SWA_TILE, row_blk_off=0,
                       n_row_blks=T_p // SWA_TILE, name=f"swa_in_p{gi}_{tag}")
        o, l = _swa_attn(pg, B=B, S=S, d=d, name=f"swa_attn_p{gi}_{tag}")
        outs.append(o)
        lses.append(l)
        keep = min(window, S)
        kv = _unpermute_tail(pg, B, S, d, keep, W, 2 * W)
        new_p.append(kv.reshape(B, keep, 2, SWA_HEADS, SWA_HD))

        ps = _swa_proj(x, p["w_in"][gi], rope_cos, rope_ss, d=1, tm=n_s, row_blk_off=T_p // n_s,
                       n_row_blks=1, name=f"swa_in_s{gi}_{tag}")
        o_s, l_s = _swa_sample_attn(ps, caches[gi], d=d, window=window, name=f"swa_attn_s{gi}_{tag}")
        outs_s.append(o_s)
        lses_s.append(l_s)
        new_s.append(ps.reshape(Bd, SAMPLE_PAD, 3 * W)[:, :Ld, W:].reshape(Bd, Ld, 2, SWA_HEADS, SWA_HD))
    full = _swa_merge(outs, lses, rows=T_p, total_rows=T_all, name=f"swa_merge_p_{tag}")
    full = _swa_merge_sample(outs_s, lses_s, full, row_off=T_p, name=f"swa_merge_s_{tag}")
    x_new = _outproj_ln(full, p["w_out"], x, ln_g, ln_b, f"swa_out_{tag}")
    return x_new, new_p, new_s


def _rope_tables(T_p, S, n_s):
    half = SWA_HD // 2
    inv = ROPE_THETA ** (-2.0 * jnp.arange(half, dtype=F32) / SWA_HD)
    pos = jnp.concatenate([jnp.arange(T_p, dtype=I32) % S,
                           PAST_LEN + jnp.arange(n_s, dtype=I32) % SAMPLE_PAD]).astype(F32)
    ang = pos[:, None] * inv[None, :]
    cos, sin = jnp.cos(ang), jnp.sin(ang)
    return jnp.concatenate([cos, cos, cos, cos], 1), jnp.concatenate([-sin, sin, -sin, sin], 1)


def kernel(x_prompt, x_sample, state_lru_h, state_lru_conv, state_gdn_s, state_gdn_conv,
           cache_swa_w128, cache_swa_w512, cache_swa_w2048,
           lru_w_in, lru_b_in, lru_conv_w, lru_conv_b, lru_w_a, lru_b_a, lru_w_i, lru_b_i,
           lru_lambda, lru_w_out,
           gdn_w_in, gdn_conv_w, gdn_a_log, gdn_dt_bias, gdn_norm_w, gdn_w_out,
           swa_w_in, swa_w_out,
           ln1_g, ln1_b, ln2_g, ln2_b,
           router_w, router_b, moe_w_gate, moe_w_up, moe_w_down):
    B, S, D = x_prompt.shape
    Bd, Ld, _ = x_sample.shape
    T_p = B * S
    n_s = Bd * SAMPLE_PAD
    assert S % SWA_TILE == 0 and Ld <= SAMPLE_PAD and T_p % n_s == 0
    xs_pad = jnp.pad(x_sample, ((0, 0), (0, SAMPLE_PAD - Ld), (0, 0)))
    x = jnp.concatenate([x_prompt.reshape(T_p, D), xs_pad.reshape(n_s, D)], axis=0)

    row = lambda v: v.reshape(1, -1)
    pad8 = lambda w: jnp.pad(w, ((0, SUBLANES - w.shape[0]), (0, 0)))
    rw_pad = jnp.pad(router_w, ((0, 0), (0, LANES - N_EXPERTS)))
    rb_pad = jnp.concatenate([router_b, jnp.full((LANES - N_EXPERTS,), -1e30, F32)]).reshape(1, LANES)
    rope_cos, rope_ss = _rope_tables(T_p, S, n_s)
    caches = (cache_swa_w128, cache_swa_w512, cache_swa_w2048)

    res = {k: [] for k in ("lru_h_p", "lru_h_s", "lru_c_p", "lru_c_s", "gdn_s_p", "gdn_s_s", "gdn_c_p",
                           "gdn_c_s")}
    swa_p = [[] for _ in range(SWA_GROUPS)]
    swa_s = [[] for _ in range(SWA_GROUPS)]
    for i in range(DEPTH):
        kind, j = LAYER_KIND[i], KIND_INDEX[i]
        g1, b1 = row(ln1_g[i]), row(ln1_b[i])
        if kind == 0:
            p = dict(w_in=lru_w_in[j].astype(BF16), b_in=row(lru_b_in[j]), cw=pad8(lru_conv_w[j]),
                     cb=row(lru_conv_b[j]),
                     wai=jnp.concatenate([lru_w_a[j], lru_w_i[j]], axis=-1).astype(BF16),
                     ba=row(lru_b_a[j]), bi=row(lru_b_i[j]), lam=row(lru_lambda[j]),
                     w_out=lru_w_out[j].astype(BF16))
            x, h_p, h_s, c_p, c_s = _lru_layer(x, T_p, B, S, Bd, Ld, state_lru_h[j], state_lru_conv[j], p,
                                               g1, b1, f"l{i}")
            res["lru_h_p"].append(h_p)
            res["lru_h_s"].append(h_s)
            res["lru_c_p"].append(c_p)
            res["lru_c_s"].append(c_s)
        elif kind == 1:
            QK, VD = GDN_HK * GDN_DK, GDN_HV * GDN_DV
            CD = 2 * QK + VD
            w_in = gdn_w_in[j]
            p = dict(w_main=jnp.pad(w_in, ((0, 0), (0, LANES - 2 * GDN_HV))).astype(BF16),
                     b_main=jnp.zeros((1, CD + VD + LANES), F32),
                     cw=pad8(gdn_conv_w[j]),
                     alog=jnp.pad(gdn_a_log[j], (GDN_HV, LANES - 2 * GDN_HV)).reshape(1, LANES),
                     dtb=jnp.pad(gdn_dt_bias[j], (GDN_HV, LANES - 2 * GDN_HV)).reshape(1, LANES),
                     nw=row(gdn_norm_w[j]), w_out=gdn_w_out[j].astype(BF16))
            x, s_p, s_s, c_p, c_s = _gdn_layer(x, T_p, B, S, Bd, Ld, state_gdn_s[j], state_gdn_conv[j], p,
                                               g1, b1, f"l{i}")
            res["gdn_s_p"].append(s_p)
            res["gdn_s_s"].append(s_s)
            res["gdn_c_p"].append(c_p)
            res["gdn_c_s"].append(c_s)
        else:
            W = SWA_HEADS * SWA_HD
            w_in = swa_w_in[j].astype(BF16)
            p = dict(w_in=[w_in[:, gi * 3 * W:(gi + 1) * 3 * W] for gi in range(SWA_GROUPS)],
                     w_out=swa_w_out[j].astype(BF16))
            x, new_p, new_s = _swa_layer(x, T_p, B, S, Bd, Ld, tuple(c[j] for c in caches), p,
                                         rope_cos, rope_ss, g1, b1, f"l{i}")
            for gi in range(SWA_GROUPS):
                swa_p[gi].append(new_p[gi])
                swa_s[gi].append(new_s[gi])
        x = _moe_ln(x, rw_pad, rb_pad, moe_w_gate, moe_w_up, moe_w_down, i,
                    row(ln2_g[i]), row(ln2_b[i]), f"l{i}")

    y_prompt = x[:T_p].reshape(B, S, D)
    y_sample = x[T_p:].reshape(Bd, SAMPLE_PAD, D)[:, :Ld]
    st = lambda k: jnp.stack(res[k])
    return (y_prompt, y_sample, st("lru_h_p"), st("lru_h_s"), st("lru_c_p"), st("lru_c_s"),
            st("gdn_s_p"), st("gdn_s_s"), st("gdn_c_p"), st("gdn_c_s"),
            jnp.stack(swa_p[0]), jnp.stack(swa_s[0]), jnp.stack(swa_p[1]), jnp.stack(swa_s[1]),
            jnp.stack(swa_p[2]), jnp.stack(swa_s[2]))
```
